```python
import math
import jax, jax.numpy as jnp
from jax import lax
import numpy as np

D_MODEL = 1024
BATCH = 2
SEQ = 16384
DEPTH = 4
DEC_BATCH = 8
DEC_SEQ = 16
PAST_LEN = 1024

CHUNK = 64
N_MIXERS = 3
N_SB = (DEPTH + 2) // 3
N_BAND = (DEPTH + 1) // 3
N_MLA = DEPTH // 3
Q_BLOCK = 128
SB_HEADS = 8
SB_HEAD_DIM = D_MODEL // SB_HEADS
BAND_HEADS = 16
BAND_HEAD_DIM = D_MODEL // BAND_HEADS
BAND_CHUNKS = 8
BAND_ROWS = BAND_CHUNKS * CHUNK
REL_MAX = 128
N_REL = 2 * REL_MAX + 1
MLA_HEADS = 8
MLA_Q_LORA = 384
MLA_KV_LORA = 256
MLA_NOPE = 128
MLA_ROPE = 64
MLA_V = 128
MLA_SCALE = (MLA_NOPE + MLA_ROPE) ** -0.5
ROPE_THETA = 10000.0
D_FF = -(-8 * D_MODEL // 768) * 256
EPS = 1e-6
FAR_POS = 2 ** 30

kernel_name = 'hybrid_streaming_sb_band_mla_step'


def rmsnorm(x, g):
    xf = x.astype(jnp.float32)
    y = xf * lax.rsqrt(jnp.mean(xf * xf, axis=-1, keepdims=True) + EPS)
    return (y * g.astype(jnp.float32)).astype(x.dtype)


def swiglu(h, w_gate_up, w_down):
    gate, up = jnp.split(h @ w_gate_up, 2, axis=-1)
    return (jax.nn.silu(gate) * up) @ w_down


def split_qkv(h, w_qkv, n_heads, head_dim):
    b, t, _ = h.shape
    qkv = (h @ w_qkv).reshape(b, t, 3, n_heads, head_dim)
    return qkv[:, :, 0], qkv[:, :, 1], qkv[:, :, 2]


def to_blocks(x, size):
    b, t = x.shape[:2]
    return jnp.moveaxis(x.reshape((b, t // size, size) + x.shape[2:]), 1, 0)


def from_blocks(x):
    x = jnp.moveaxis(x, 0, 1)
    return x.reshape((x.shape[0], x.shape[1] * x.shape[2]) + x.shape[3:])


def sb_attend(q, k, v, q_pos, k_pos):
    tk = k.shape[1]
    pad = (-tk) % Q_BLOCK
    if pad:
        k = jnp.pad(k, ((0, 0), (0, pad), (0, 0), (0, 0)))
        v = jnp.pad(v, ((0, 0), (0, pad), (0, 0), (0, 0)))
        k_pos = jnp.concatenate([k_pos, jnp.full((pad,), FAR_POS, k_pos.dtype)])
    b, tq, h, dh = q.shape
    nk = (tk + pad) // Q_BLOCK
    z = jnp.einsum('bqhd,bkhd->bhqk', q, k).astype(jnp.float32) * (dh ** -0.5)
    mask = k_pos[None, :] < q_pos[:, None]
    log_stay = jnp.where(mask, jax.nn.log_sigmoid(-z), 0.0)
    incl = jnp.tril(jnp.ones((Q_BLOCK, Q_BLOCK), jnp.float32))
    r_in = jnp.einsum('bhqnj,jl->bhqnl', log_stay.reshape(b, h, tq, nk, Q_BLOCK), incl)
    blk_tot = r_in[..., 0]
    after = lax.cumsum(blk_tot, axis=3, reverse=True) - blk_tot
    log_w = (z.reshape(b, h, tq, nk, Q_BLOCK) + r_in + after[..., None]).reshape(b, h, tq, nk * Q_BLOCK)
    w = jnp.exp(jnp.where(mask, log_w, -jnp.inf))
    return jnp.einsum('bhqk,bkhd->bqhd', w.astype(v.dtype), v)


def sb_prompt(h, w_qkv, w_o):
    b, s, _ = h.shape
    q, k, v = split_qkv(h, w_qkv, SB_HEADS, SB_HEAD_DIM)
    outs = []
    for i in range(s // Q_BLOCK):
        start, end = i * Q_BLOCK, (i + 1) * Q_BLOCK
        outs.append(sb_attend(q[:, start:end], k[:, :end], v[:, :end],
                              jnp.arange(start, end), jnp.arange(end)))
    o = jnp.concatenate(outs, axis=1)
    return o.reshape(b, s, D_MODEL) @ w_o, k, v


def sb_sample(h, cache_k, cache_v, w_qkv, w_o):
    b, t, _ = h.shape
    p = cache_k.shape[1]
    q, k, v = split_qkv(h, w_qkv, SB_HEADS, SB_HEAD_DIM)
    o = sb_attend(q, jnp.concatenate([cache_k, k], axis=1), jnp.concatenate([cache_v, v], axis=1),
                  p + jnp.arange(t), jnp.arange(p + t))
    return o.reshape(b, t, D_MODEL) @ w_o, k, v


def band_attend(q, k, v, q_pos, k_pos, rel_bias):
    s = jnp.einsum('bqhd,bkhd->bhqk', q, k).astype(jnp.float32) * (BAND_HEAD_DIM ** -0.5)
    rel = jnp.clip(q_pos[:, None] - k_pos[None, :], -REL_MAX, REL_MAX) + REL_MAX
    s = s + rel_bias[:, rel].astype(jnp.float32)[None]
    q_chunk = (q_pos // CHUNK)[:, None]
    k_chunk = (k_pos // CHUNK)[None, :]
    mask = (k_pos[None, :] >= 0) & (k_chunk <= q_chunk) & (k_chunk >= q_chunk - BAND_CHUNKS)
    p = jax.nn.softmax(jnp.where(mask, s, -jnp.inf), axis=-1)
    return jnp.einsum('bhqk,bkhd->bqhd', p.astype(v.dtype), v)


def band_prompt(h, w_qkv, rel_bias, w_o):
    b, s, _ = h.shape
    q, k, v = split_qkv(h, w_qkv, BAND_HEADS, BAND_HEAD_DIM)
    pad = ((0, 0), (BAND_ROWS, 0), (0, 0), (0, 0))
    k_pad, v_pad = jnp.pad(k, pad), jnp.pad(v, pad)
    band_len = BAND_ROWS + CHUNK

    def chunk(args):
        q_chk, c = args
        start = c * CHUNK
        k_band = lax.dynamic_slice_in_dim(k_pad, start, band_len, axis=1)
        v_band = lax.dynamic_slice_in_dim(v_pad, start, band_len, axis=1)
        return band_attend(q_chk, k_band, v_band, start + jnp.arange(CHUNK),
                           start - BAND_ROWS + jnp.arange(band_len), rel_bias)

    o = from_blocks(lax.map(chunk, (to_blocks(q, CHUNK), jnp.arange(s // CHUNK))))
    keep = min(BAND_ROWS, s)
    return o.reshape(b, s, D_MODEL) @ w_o, k[:, s - keep:], v[:, s - keep:]


def band_sample(h, cache_k, cache_v, w_qkv, rel_bias, w_o, past_len):
    b, t, _ = h.shape
    rows = cache_k.shape[1]
    q, k, v = split_qkv(h, w_qkv, BAND_HEADS, BAND_HEAD_DIM)
    q_pos = past_len + jnp.arange(t)
    k_pos = jnp.concatenate([past_len - rows + jnp.arange(rows), q_pos])
    o = band_attend(q, jnp.concatenate([cache_k, k], axis=1), jnp.concatenate([cache_v, v], axis=1),
                    q_pos, k_pos, rel_bias)
    return o.reshape(b, t, D_MODEL) @ w_o, k, v


def rope_cos_sin(pos):
    half = MLA_ROPE // 2
    inv_freq = ROPE_THETA ** (-jnp.arange(half, dtype=jnp.float32) / half)
    ang = pos.astype(jnp.float32)[:, None] * inv_freq[None, :]
    return jnp.cos(ang), jnp.sin(ang)


def apply_rope(x, cos, sin):
    half = x.shape[-1] // 2
    xf = x.astype(jnp.float32)
    x1, x2 = xf[..., :half], xf[..., half:]
    return jnp.concatenate([x1 * cos - x2 * sin, x2 * cos + x1 * sin], axis=-1).astype(x.dtype)


def mla_project(h, pos, w_dq, q_norm, w_uq, w_dkv, kv_norm):
    b, t, _ = h.shape
    q = (rmsnorm(h @ w_dq, q_norm) @ w_uq).reshape(b, t, MLA_HEADS, MLA_NOPE + MLA_ROPE)
    kv = h @ w_dkv
    c_kv = rmsnorm(kv[..., :MLA_KV_LORA], kv_norm)
    cos, sin = rope_cos_sin(pos)
    q_rope = apply_rope(q[..., MLA_NOPE:], cos[:, None, :], sin[:, None, :])
    k_rope = apply_rope(kv[..., MLA_KV_LORA:], cos, sin)
    return q[..., :MLA_NOPE], q_rope, c_kv, k_rope


def chunk_softmax(s, q_pos, k_pos):
    mask = (k_pos // CHUNK)[None, :] <= (q_pos // CHUNK)[:, None]
    return jax.nn.softmax(jnp.where(mask, s, -jnp.inf), axis=-1)


def mla_prompt(h, w_dq, q_norm, w_uq, w_dkv, kv_norm, w_uk, w_uv, w_o):
    b, s, _ = h.shape
    pos = jnp.arange(s)
    q_nope, q_rope, c_kv, k_rope = mla_project(h, pos, w_dq, q_norm, w_uq, w_dkv, kv_norm)
    k_nope = jnp.einsum('bkc,chn->bkhn', c_kv, w_uk)
    v = jnp.einsum('bkc,chv->bkhv', c_kv, w_uv)
    outs = []
    for i in range(s // Q_BLOCK):
        start, end = i * Q_BLOCK, (i + 1) * Q_BLOCK
        sc = (jnp.einsum('bqhn,bkhn->bhqk', q_nope[:, start:end], k_nope[:, :end]).astype(jnp.float32)
              + jnp.einsum('bqhr,bkr->bhqk', q_rope[:, start:end], k_rope[:, :end]).astype(jnp.float32)) * MLA_SCALE
        p = chunk_softmax(sc, jnp.arange(start, end), jnp.arange(end))
        outs.append(jnp.einsum('bhqk,bkhv->bqhv', p.astype(v.dtype), v[:, :end]))
    o = jnp.concatenate(outs, axis=1)
    return o.reshape(b, s, MLA_HEADS * MLA_V) @ w_o, c_kv, k_rope


def mla_sample(h, cache_ckv, cache_krope, w_dq, q_norm, w_uq, w_dkv, kv_norm, w_uk, w_uv, w_o):
    b, t, _ = h.shape
    p = cache_ckv.shape[1]
    q_pos = p + jnp.arange(t)
    q_nope, q_rope, c_kv, k_rope = mla_project(h, q_pos, w_dq, q_norm, w_uq, w_dkv, kv_norm)
    c_all = jnp.concatenate([cache_ckv, c_kv], axis=1)
    r_all = jnp.concatenate([cache_krope, k_rope], axis=1)
    q_lat = jnp.einsum('bqhn,chn->bqhc', q_nope, w_uk)
    sc = (jnp.einsum('bqhc,bkc->bhqk', q_lat, c_all).astype(jnp.float32)
          + jnp.einsum('bqhr,bkr->bhqk', q_rope, r_all).astype(jnp.float32)) * MLA_SCALE
    prob = chunk_softmax(sc, q_pos, jnp.arange(p + t))
    o_lat = jnp.einsum('bhqk,bkc->bqhc', prob.astype(c_all.dtype), c_all)
    o = jnp.einsum('bqhc,chv->bqhv', o_lat, w_uv)
    return o.reshape(b, t, MLA_HEADS * MLA_V) @ w_o, c_kv, k_rope


def setup_inputs(seed: int = 0) -> dict:
    key = jax.random.key(seed)
    keys = iter(jax.random.split(key, 32))

    def nrm(shape, scale):
        return jax.random.normal(next(keys), shape, jnp.float32) * scale

    band_rows = min(BAND_ROWS, PAST_LEN)
    D = D_MODEL
    return {
        'x_prompt': nrm((BATCH, SEQ, D), 1.0),
        'x_sample': nrm((DEC_BATCH, DEC_SEQ, D), 1.0),
        'cache_sb_k': nrm((N_SB, DEC_BATCH, PAST_LEN, SB_HEADS, SB_HEAD_DIM), 1.0),
        'cache_sb_v': nrm((N_SB, DEC_BATCH, PAST_LEN, SB_HEADS, SB_HEAD_DIM), 1.0),
        'cache_band_k': nrm((N_BAND, DEC_BATCH, band_rows, BAND_HEADS, BAND_HEAD_DIM), 1.0),
        'cache_band_v': nrm((N_BAND, DEC_BATCH, band_rows, BAND_HEADS, BAND_HEAD_DIM), 1.0),
        'cache_mla_ckv': nrm((N_MLA, DEC_BATCH, PAST_LEN, MLA_KV_LORA), 1.0),
        'cache_mla_krope': nrm((N_MLA, DEC_BATCH, PAST_LEN, MLA_ROPE), 1.0),
        'norm_mix': 1.0 + nrm((DEPTH, D), 0.05),
        'norm_ffn': 1.0 + nrm((DEPTH, D), 0.05),
        'norm_final': 1.0 + nrm((D,), 0.05),
        'w_gate_up': nrm((DEPTH, D, 2 * D_FF), D ** -0.5),
        'w_down': nrm((DEPTH, D_FF, D), D_FF ** -0.5),
        'sb_w_qkv': nrm((N_SB, D, 3 * D), D ** -0.5),
        'sb_w_o': nrm((N_SB, D, D), D ** -0.5),
        'band_w_qkv': nrm((N_BAND, D, 3 * D), D ** -0.5),
        'band_rel_bias': nrm((N_BAND, BAND_HEADS, N_REL), 0.2),
        'band_w_o': nrm((N_BAND, D, D), D ** -0.5),
        'mla_w_dq': nrm((N_MLA, D, MLA_Q_LORA), D ** -0.5),
        'mla_q_norm': 1.0 + nrm((N_MLA, MLA_Q_LORA), 0.05),
        'mla_w_uq': nrm((N_MLA, MLA_Q_LORA, MLA_HEADS * (MLA_NOPE + MLA_ROPE)), MLA_Q_LORA ** -0.5),
        'mla_w_dkv': nrm((N_MLA, D, MLA_KV_LORA + MLA_ROPE), D ** -0.5),
        'mla_kv_norm': 1.0 + nrm((N_MLA, MLA_KV_LORA), 0.05),
        'mla_w_uk': nrm((N_MLA, MLA_KV_LORA, MLA_HEADS, MLA_NOPE), MLA_KV_LORA ** -0.5),
        'mla_w_uv': nrm((N_MLA, MLA_KV_LORA, MLA_HEADS, MLA_V), MLA_KV_LORA ** -0.5),
        'mla_w_o': nrm((N_MLA, MLA_HEADS * MLA_V, D), (MLA_HEADS * MLA_V) ** -0.5),
    }


def reference(x_prompt, x_sample, cache_sb_k, cache_sb_v, cache_band_k, cache_band_v,
              cache_mla_ckv, cache_mla_krope, norm_mix, norm_ffn, norm_final, w_gate_up, w_down,
              sb_w_qkv, sb_w_o, band_w_qkv, band_rel_bias, band_w_o,
              mla_w_dq, mla_q_norm, mla_w_uq, mla_w_dkv, mla_kv_norm, mla_w_uk, mla_w_uv, mla_w_o):
    past_len = cache_sb_k.shape[2]
    xp, xs = x_prompt, x_sample
    sb_kp, sb_vp, sb_ks, sb_vs = [], [], [], []
    band_kp, band_vp, band_ks, band_vs = [], [], [], []
    mla_cp, mla_rp, mla_cs, mla_rs = [], [], [], []
    for i in range(DEPTH):
        kind, j = i % N_MIXERS, i // N_MIXERS
        hp = rmsnorm(xp, norm_mix[i])
        hs = rmsnorm(xs, norm_mix[i])
        if kind == 0:
            mp, kp_, vp_ = sb_prompt(hp, sb_w_qkv[j], sb_w_o[j])
            ms, ks_, vs_ = sb_sample(hs, cache_sb_k[j], cache_sb_v[j], sb_w_qkv[j], sb_w_o[j])
            sb_kp.append(kp_); sb_vp.append(vp_); sb_ks.append(ks_); sb_vs.append(vs_)
        elif kind == 1:
            mp, kp_, vp_ = band_prompt(hp, band_w_qkv[j], band_rel_bias[j], band_w_o[j])
            ms, ks_, vs_ = band_sample(hs, cache_band_k[j], cache_band_v[j], band_w_qkv[j],
                                       band_rel_bias[j], band_w_o[j], past_len)
            band_kp.append(kp_); band_vp.append(vp_); band_ks.append(ks_); band_vs.append(vs_)
        else:
            mp, cp_, rp_ = mla_prompt(hp, mla_w_dq[j], mla_q_norm[j], mla_w_uq[j], mla_w_dkv[j],
                                      mla_kv_norm[j], mla_w_uk[j], mla_w_uv[j], mla_w_o[j])
            ms, cs_, rs_ = mla_sample(hs, cache_mla_ckv[j], cache_mla_krope[j], mla_w_dq[j], mla_q_norm[j],
                                      mla_w_uq[j], mla_w_dkv[j], mla_kv_norm[j], mla_w_uk[j],
                                      mla_w_uv[j], mla_w_o[j])
            mla_cp.append(cp_); mla_rp.append(rp_); mla_cs.append(cs_); mla_rs.append(rs_)
        xp = xp + mp
        xs = xs + ms
        xp = xp + swiglu(rmsnorm(xp, norm_ffn[i]), w_gate_up[i], w_down[i])
        xs = xs + swiglu(rmsnorm(xs, norm_ffn[i]), w_gate_up[i], w_down[i])
    y_prompt = rmsnorm(xp, norm_final)
    y_sample = rmsnorm(xs, norm_final)
    return (y_prompt, y_sample,
            jnp.stack(sb_kp), jnp.stack(sb_vp), jnp.stack(band_kp), jnp.stack(band_vp),
            jnp.stack(mla_cp), jnp.stack(mla_rp),
            jnp.stack(sb_ks), jnp.stack(sb_vs), jnp.stack(band_ks), jnp.stack(band_vs),
            jnp.stack(mla_cs), jnp.stack(mla_rs))
```

```python
import functools

import jax
import jax.numpy as jnp
from jax import lax
from jax.experimental import pallas as pl
from jax.experimental.pallas import tpu as pltpu

CHUNK = 64
BAND_CHUNKS = 8
REL_MAX = 128
MLA_NOPE = 128
MLA_ROPE = 64
MLA_V = 128
ROPE_THETA = 10000.0
EPS = 1e-6

SB_LOG_ZERO = -104.0

VMEM_LIMIT = 56 * 1024 * 1024
F32 = jnp.float32
BF16 = jnp.bfloat16


def _cparams(n_grid):
    return pltpu.CompilerParams(dimension_semantics=("arbitrary",) * n_grid, vmem_limit_bytes=VMEM_LIMIT)


def _rms(x, g):
    return x * lax.rsqrt(jnp.mean(x * x, axis=-1, keepdims=True) + EPS) * g


def _dot(a, b):
    return jnp.dot(a, b, preferred_element_type=F32)


def _dot_nt(a, b):
    return lax.dot_general(a, b, (((1,), (1,)), ((), ())), preferred_element_type=F32)


def _const_spec(shape):
    nd = len(shape)
    return pl.BlockSpec(shape, lambda *_: (0,) * nd)


def _qkv_kernel(x_ref, g_ref, w_ref, q_ref, k_ref, v_ref, kb_ref, vb_ref, *, d, q_scale):
    h = _rms(x_ref[...], g_ref[...]).astype(BF16)
    q = _dot(h, w_ref[:, 0:d])
    q_ref[...] = (q * q_scale).astype(BF16)
    k = _dot(h, w_ref[:, d:2 * d])
    k_ref[...] = k
    kb_ref[...] = k.astype(BF16)
    v = _dot(h, w_ref[:, 2 * d:3 * d])
    v_ref[...] = v
    vb_ref[...] = v.astype(BF16)


def _qkv_proj(x2d, g, w_bf, q_scale, tm):
    n, d = x2d.shape
    row = lambda i: (i, 0)
    return pl.pallas_call(
        functools.partial(_qkv_kernel, d=d, q_scale=q_scale),
        grid=(n // tm,),
        in_specs=[pl.BlockSpec((tm, d), row), _const_spec((1, d)), _const_spec((d, 3 * d))],
        out_specs=[pl.BlockSpec((tm, d), row)] * 5,
        out_shape=[jax.ShapeDtypeStruct((n, d), BF16), jax.ShapeDtypeStruct((n, d), F32),
                   jax.ShapeDtypeStruct((n, d), F32), jax.ShapeDtypeStruct((n, d), BF16),
                   jax.ShapeDtypeStruct((n, d), BF16)],
        compiler_params=_cparams(1),
        name="qkv_proj",
    )(x2d, g.reshape(1, d), w_bf)


def _ffn_kernel(x_ref, a_ref, wo_ref, g_ref, wgu_ref, wd_ref, gf_ref, o_ref, acc_ref, h_ref, *, n_chunks, fc, final):
    x1 = x_ref[...] + _dot(a_ref[...], wo_ref[...])
    h_ref[...] = _rms(x1, g_ref[...]).astype(BF16)
    acc_ref[...] = x1

    def body(c, carry):
        gu = _dot(h_ref[...], wgu_ref[c])
        gate = gu[:, :fc]
        up = gu[:, fc:]
        act = (gate / (1.0 + jnp.exp(-gate)) * up).astype(BF16)
        acc_ref[...] += _dot(act, wd_ref[c])
        return carry

    lax.fori_loop(0, n_chunks, body, 0)
    out = acc_ref[...]
    if final:
        out = _rms(out, gf_ref[...])
    o_ref[...] = out


def _oproj_ffn(x2d, a2d, wo_bf, g, wgu_c, wd_c, g_final, final, tm):
    n, d = x2d.shape
    n_chunks, _, fc2 = wgu_c.shape
    fc = fc2 // 2
    row = lambda i: (i, 0)
    return pl.pallas_call(
        functools.partial(_ffn_kernel, n_chunks=n_chunks, fc=fc, final=final),
        grid=(n // tm,),
        in_specs=[pl.BlockSpec((tm, d), row), pl.BlockSpec((tm, d), row), _const_spec((d, d)), _const_spec((1, d)),
                  _const_spec(wgu_c.shape), _const_spec(wd_c.shape), _const_spec((1, d))],
        out_specs=pl.BlockSpec((tm, d), row),
        out_shape=jax.ShapeDtypeStruct((n, d), F32),
        scratch_shapes=[pltpu.VMEM((tm, d), F32), pltpu.VMEM((tm, d), BF16)],
        compiler_params=_cparams(1),
        name="oproj_ffn",
    )(x2d, a2d, wo_bf, g.reshape(1, d), wgu_c, wd_c, g_final.reshape(1, d))


def _sb_kernel(q_ref, k_ref, v_ref, tri_ref, o_ref, *, tq, tk, q_off):
    i = pl.program_id(2)
    q = q_ref[...]
    q_start = q_off + i * tq
    q_pos = q_start + lax.broadcasted_iota(jnp.int32, (tq, 1), 0)
    col = lax.broadcasted_iota(jnp.int32, (1, tk), 1)
    tri = tri_ref[...]
    j_last = (q_start + tq - 2) // tk

    def cond(state):
        j, a_max, _, _ = state
        return jnp.logical_and(j >= 0, a_max > SB_LOG_ZERO)

    def body(state):
        j, _, a, acc = state
        start = pl.multiple_of(j * tk, tk)
        kb = k_ref[pl.ds(start, tk), :]
        vb = v_ref[pl.ds(start, tk), :]
        z = _dot_nt(q, kb)
        mask = (start + col) < q_pos
        ls = jnp.where(mask, jnp.minimum(-z, 0.0) - jnp.log(1.0 + jnp.exp(-jnp.abs(z))), 0.0)
        hi = ls.astype(BF16)
        lo = (ls - hi.astype(F32)).astype(BF16)
        r_in = _dot(hi, tri) + _dot(lo, tri)
        w = jnp.where(mask, jnp.exp(z + r_in + a), 0.0)
        acc = acc + _dot(w.astype(BF16), vb)
        a = a + r_in[:, 0:1]
        return j - 1, jnp.max(a), a, acc

    init = (j_last, jnp.float32(0.0), jnp.zeros((tq, 1), F32), jnp.zeros((tq, q.shape[1]), F32))
    _, _, _, acc = lax.while_loop(cond, body, init)
    o_ref[...] = acc.astype(o_ref.dtype)


def _sb_attention(q, k, v, heads, tq, tk, q_off):
    b, t_q, d = q.shape
    t_k = k.shape[1]
    dh = d // heads
    tri = jnp.tril(jnp.ones((tk, tk), F32)).astype(BF16)
    return pl.pallas_call(
        functools.partial(_sb_kernel, tq=tq, tk=tk, q_off=q_off),
        grid=(b, heads, t_q // tq),
        in_specs=[pl.BlockSpec((None, tq, dh), lambda b_, h, i: (b_, i, h)),
                  pl.BlockSpec((None, t_k, dh), lambda b_, h, i: (b_, 0, h)),
                  pl.BlockSpec((None, t_k, dh), lambda b_, h, i: (b_, 0, h)),
                  _const_spec((tk, tk))],
        out_specs=pl.BlockSpec((None, tq, dh), lambda b_, h, i: (b_, i, h)),
        out_shape=jax.ShapeDtypeStruct((b, t_q, d), BF16),
        compiler_params=_cparams(3),
        name="sb_attention",
    )(q, k, v, tri)


def _band_kernel(q_ref, k0_ref, k1_ref, k2_ref, v0_ref, v1_ref, v2_ref, tile_ref, o_ref, *, tk, prompt):
    q = q_ref[...]
    kw = jnp.concatenate([k0_ref[...], k1_ref[...], k2_ref[...]], axis=0)
    vw = jnp.concatenate([v0_ref[...], v1_ref[...], v2_ref[...]], axis=0)
    half = q.shape[1] // 2
    lane = lax.broadcasted_iota(jnp.int32, (1, q.shape[1]), 1)
    col = lax.broadcasted_iota(jnp.int32, (1, 3 * tk), 1)
    first_col = (2 - pl.program_id(2)) * tk if prompt else 0
    outs = []
    for hh in range(2):
        in_head = (lane >= hh * half) & (lane < (hh + 1) * half)
        s = _dot_nt(jnp.where(in_head, q, jnp.zeros_like(q)), kw) + tile_ref[hh]
        s = jnp.where(col >= first_col, s, -jnp.inf)
        m = jnp.max(s, axis=-1, keepdims=True)
        p = jnp.exp(s - m)
        l = jnp.sum(p, axis=-1, keepdims=True)
        outs.append(_dot(p.astype(BF16), vw) / l)
    o_ref[...] = jnp.where(lane < half, outs[0], outs[1]).astype(o_ref.dtype)


def _band_tile(rel_bias, q_pos, k_pos, valid):
    rel = jnp.clip(q_pos[:, None] - k_pos[None, :], -REL_MAX, REL_MAX) + REL_MAX
    q_chunk = (q_pos // CHUNK)[:, None]
    k_chunk = (k_pos // CHUNK)[None, :]
    mask = valid[None, :] & (k_chunk <= q_chunk) & (k_chunk >= q_chunk - BAND_CHUNKS)
    return jnp.where(mask[None], rel_bias[:, rel].astype(F32), -jnp.inf)


def _band_attention(q, k, v, tile, heads, tq, tk, prompt):
    b, t_q, d = q.shape
    pair = 2 * (d // heads)
    if prompt:
        kmap = lambda t: (lambda hp, b_, i: (b_, jnp.maximum(i - 2 + t, 0), hp))
    else:
        kmap = lambda t: (lambda hp, b_, i: (b_, t, hp))
    kspecs = [pl.BlockSpec((None, tk, pair), kmap(t)) for t in range(3)]
    return pl.pallas_call(
        functools.partial(_band_kernel, tk=tk, prompt=prompt),
        grid=(heads // 2, b, t_q // tq),
        in_specs=[pl.BlockSpec((None, tq, pair), lambda hp, b_, i: (b_, i, hp))] + kspecs + kspecs
        + [pl.BlockSpec((2, tq, 3 * tk), lambda hp, b_, i: (hp, 0, 0))],
        out_specs=pl.BlockSpec((None, tq, pair), lambda hp, b_, i: (b_, i, hp)),
        out_shape=jax.ShapeDtypeStruct((b, t_q, d), BF16),
        compiler_params=_cparams(3),
        name="band_attention",
    )(q, k, k, k, v, v, v, tile)


def _mla_proj_kernel(x_ref, g_ref, wdq_ref, qn_ref, wqc_ref, wqs_ref, wdkv_ref, kvn_ref, wuk_ref, wuv_ref,
                     cos_ref, sin_ref, qc_ref, kc_ref, v_ref, ckv_ref, kr_ref, *, heads, scale):
    w = MLA_NOPE
    h = _rms(x_ref[...], g_ref[...]).astype(BF16)
    qn = _rms(_dot(h, wdq_ref[...]), qn_ref[...]).astype(BF16)
    a = _dot(qn, wqc_ref[...])
    sw = _dot(qn, wqs_ref[...])
    cos = cos_ref[...]
    sin = sin_ref[...]
    for hd in range(heads):
        qc_ref[:, 2 * w * hd:2 * w * hd + w] = (a[:, 2 * w * hd:2 * w * hd + w] * scale).astype(BF16)
        rope = a[:, 2 * w * hd + w:2 * w * (hd + 1)] * cos + sw[:, w * hd:w * (hd + 1)] * sin
        qc_ref[:, 2 * w * hd + w:2 * w * (hd + 1)] = (rope * scale).astype(BF16)
    kv = _dot(h, wdkv_ref[...])
    lora = kv.shape[1] - 2 * w
    ckv = _rms(kv[:, :lora], kvn_ref[...])
    ckv_ref[...] = ckv
    kr = kv[:, lora:lora + w] * cos + kv[:, lora + w:lora + 2 * w] * sin
    kr_ref[...] = kr[:, :MLA_ROPE]
    cb = ckv.astype(BF16)
    kn = _dot(cb, wuk_ref[...])
    v_ref[...] = _dot(cb, wuv_ref[...]).astype(BF16)
    krb = kr.astype(BF16)
    for hd in range(heads):
        kc_ref[:, 2 * w * hd:2 * w * hd + w] = kn[:, w * hd:w * (hd + 1)].astype(BF16)
        kc_ref[:, 2 * w * hd + w:2 * w * (hd + 1)] = krb


def _mla_proj(x2d, g, wts, cos_t, sin_t, heads, scale, tm):
    n, d = x2d.shape
    wdq, qn, wqc, wqs, wdkv, kvn, wuk, wuv = wts
    lora = wuk.shape[0]
    n_pos = cos_t.shape[0] // tm
    row = lambda i: (i, 0)
    pos = lambda i: (i % n_pos, 0)
    return pl.pallas_call(
        functools.partial(_mla_proj_kernel, heads=heads, scale=scale),
        grid=(n // tm,),
        in_specs=[pl.BlockSpec((tm, d), row), _const_spec((1, d)), _const_spec(wdq.shape), _const_spec(qn.shape),
                  _const_spec(wqc.shape), _const_spec(wqs.shape), _const_spec(wdkv.shape), _const_spec(kvn.shape),
                  _const_spec(wuk.shape), _const_spec(wuv.shape),
                  pl.BlockSpec((tm, MLA_NOPE), pos), pl.BlockSpec((tm, MLA_NOPE), pos)],
        out_specs=[pl.BlockSpec((tm, heads * 2 * MLA_NOPE), row), pl.BlockSpec((tm, heads * 2 * MLA_NOPE), row),
                   pl.BlockSpec((tm, heads * MLA_V), row), pl.BlockSpec((tm, lora), row),
                   pl.BlockSpec((tm, MLA_ROPE), row)],
        out_shape=[jax.ShapeDtypeStruct((n, heads * 2 * MLA_NOPE), BF16),
                   jax.ShapeDtypeStruct((n, heads * 2 * MLA_NOPE), BF16),
                   jax.ShapeDtypeStruct((n, heads * MLA_V), BF16),
                   jax.ShapeDtypeStruct((n, lora), F32),
                   jax.ShapeDtypeStruct((n, MLA_ROPE), F32)],
        compiler_params=_cparams(1),
        name="mla_proj",
    )(x2d, g.reshape(1, d), wdq, qn, wqc, wqs, wdkv, kvn, wuk, wuv, cos_t, sin_t)


def _mla_expand_kernel(c_ref, r_ref, wuk_ref, wuv_ref, kc_ref, v_ref, *, heads):
    w = MLA_NOPE
    cb = c_ref[...].astype(BF16)
    kn = _dot(cb, wuk_ref[...])
    v_ref[...] = _dot(cb, wuv_ref[...]).astype(BF16)
    krb = r_ref[...].astype(BF16)
    zeros = jnp.zeros((krb.shape[0], w - MLA_ROPE), BF16)
    for hd in range(heads):
        kc_ref[:, 2 * w * hd:2 * w * hd + w] = kn[:, w * hd:w * (hd + 1)].astype(BF16)
        kc_ref[:, 2 * w * hd + w:2 * w * hd + w + MLA_ROPE] = krb
        kc_ref[:, 2 * w * hd + w + MLA_ROPE:2 * w * (hd + 1)] = zeros


def _mla_expand(c2d, r2d, wuk, wuv, heads, tm):
    n, lora = c2d.shape
    row = lambda i: (i, 0)
    return pl.pallas_call(
        functools.partial(_mla_expand_kernel, heads=heads),
        grid=(n // tm,),
        in_specs=[pl.BlockSpec((tm, lora), row), pl.BlockSpec((tm, MLA_ROPE), row),
                  _const_spec(wuk.shape), _const_spec(wuv.shape)],
        out_specs=[pl.BlockSpec((tm, heads * 2 * MLA_NOPE), row), pl.BlockSpec((tm, heads * MLA_V), row)],
        out_shape=[jax.ShapeDtypeStruct((n, heads * 2 * MLA_NOPE), BF16),
                   jax.ShapeDtypeStruct((n, heads * MLA_V), BF16)],
        compiler_params=_cparams(1),
        name="mla_expand",
    )(c2d, r2d, wuk, wuv)


def _mla_attn_kernel(q_ref, k_ref, v_ref, o_ref, *, tq, tk, q_off, kv_len):
    i = pl.program_id(2)
    q = q_ref[...]
    q_start = q_off + i * tq
    q_chunk = (q_start + lax.broadcasted_iota(jnp.int32, (tq, 1), 0)) // CHUNK
    col = lax.broadcasted_iota(jnp.int32, (1, tk), 1)
    n_full = jnp.minimum((q_start // CHUNK + 1) * CHUNK, kv_len) // tk
    k_end = jnp.minimum(((q_start + tq - 1) // CHUNK + 1) * CHUNK, kv_len)
    n_blk = (k_end + tk - 1) // tk

    def step(j, state, masked):
        m, l, acc = state
        start = pl.multiple_of(j * tk, tk)
        s = _dot_nt(q, k_ref[pl.ds(start, tk), :])
        if masked:
            k_pos = start + col
            s = jnp.where(((k_pos // CHUNK) <= q_chunk) & (k_pos < kv_len), s, -jnp.inf)
        m_new = jnp.maximum(m, jnp.max(s, axis=-1, keepdims=True))
        alpha = jnp.exp(m - m_new)
        p = jnp.exp(s - m_new)
        l = alpha * l + jnp.sum(p, axis=-1, keepdims=True)
        acc = alpha * acc + _dot(p.astype(BF16), v_ref[pl.ds(start, tk), :])
        return m_new, l, acc

    state = (jnp.full((tq, 1), -jnp.inf, F32), jnp.zeros((tq, 1), F32), jnp.zeros((tq, v_ref.shape[1]), F32))
    state = lax.fori_loop(0, n_full, functools.partial(step, masked=False), state)
    _, l, acc = lax.fori_loop(n_full, n_blk, functools.partial(step, masked=True), state)
    o_ref[...] = (acc / l).astype(o_ref.dtype)


def _mla_attention(qc, kc, v, heads, tq, tk, q_off, kv_len):
    b, t_q, _ = qc.shape
    t_k = kc.shape[1]
    return pl.pallas_call(
        functools.partial(_mla_attn_kernel, tq=tq, tk=tk, q_off=q_off, kv_len=kv_len),
        grid=(b, heads, t_q // tq),
        in_specs=[pl.BlockSpec((None, tq, 2 * MLA_NOPE), lambda b_, h, i: (b_, i, h)),
                  pl.BlockSpec((None, t_k, 2 * MLA_NOPE), lambda b_, h, i: (b_, 0, h)),
                  pl.BlockSpec((None, t_k, MLA_V), lambda b_, h, i: (b_, 0, h))],
        out_specs=pl.BlockSpec((None, tq, MLA_V), lambda b_, h, i: (b_, i, h)),
        out_shape=jax.ShapeDtypeStruct((b, t_q, heads * MLA_V), BF16),
        compiler_params=_cparams(3),
        name="mla_attention",
    )(qc, kc, v)


def _rope_tables(pos):
    half = MLA_ROPE // 2
    inv_freq = ROPE_THETA ** (-jnp.arange(half, dtype=F32) / half)
    ang = pos.astype(F32)[:, None] * inv_freq[None, :]
    zeros = jnp.zeros((pos.shape[0], MLA_NOPE - MLA_ROPE), F32)
    cos, sin = jnp.cos(ang), jnp.sin(ang)
    return jnp.concatenate([cos, cos, zeros], axis=1), jnp.concatenate([sin, sin, zeros], axis=1)


def _swap_halves(w):
    half = w.shape[-1] // 2
    return jnp.concatenate([-w[..., half:], w[..., :half]], axis=-1)


def _mla_weights(w_dq, q_norm, w_uq, w_dkv, kv_norm, w_uk, w_uv, heads):
    q_lora = w_dq.shape[1]
    lora = w_uk.shape[0]
    pad = MLA_NOPE - MLA_ROPE
    uq = w_uq.reshape(q_lora, heads, MLA_NOPE + MLA_ROPE)
    zq = jnp.zeros((q_lora, heads, pad), F32)
    wqc = jnp.concatenate([uq, zq], axis=-1).reshape(q_lora, heads * 2 * MLA_NOPE)
    wqs = jnp.concatenate([_swap_halves(uq[..., MLA_NOPE:]), zq], axis=-1).reshape(q_lora, heads * MLA_NOPE)
    zk = jnp.zeros((w_dkv.shape[0], pad), F32)
    rope = w_dkv[:, lora:]
    wdkv = jnp.concatenate([w_dkv[:, :lora], rope, zk, _swap_halves(rope), zk], axis=1)
    return (w_dq.astype(BF16), q_norm.reshape(1, q_lora), wqc.astype(BF16), wqs.astype(BF16), wdkv.astype(BF16),
            kv_norm.reshape(1, lora), w_uk.reshape(lora, heads * MLA_NOPE).astype(BF16),
            w_uv.reshape(lora, heads * MLA_V).astype(BF16))


def _ffn_weights(w_gate_up, w_down, fc):
    d, two_ff = w_gate_up.shape
    ff = two_ff // 2
    n_chunks = ff // fc
    gate = w_gate_up[:, :ff].reshape(d, n_chunks, fc)
    up = w_gate_up[:, ff:].reshape(d, n_chunks, fc)
    wgu = jnp.transpose(jnp.concatenate([gate, up], axis=-1), (1, 0, 2)).astype(BF16)
    return wgu, w_down.reshape(n_chunks, fc, d).astype(BF16)


def _pad_rows(x, rows):
    return jnp.pad(x, ((0, 0), (0, rows - x.shape[1]), (0, 0)))


def kernel(x_prompt, x_sample, cache_sb_k, cache_sb_v, cache_band_k, cache_band_v, cache_mla_ckv, cache_mla_krope, norm_mix, norm_ffn, norm_final, w_gate_up, w_down, sb_w_qkv, sb_w_o, band_w_qkv, band_rel_bias, band_w_o, mla_w_dq, mla_q_norm, mla_w_uq, mla_w_dkv, mla_kv_norm, mla_w_uk, mla_w_uv, mla_w_o):
    bp, seq, d = x_prompt.shape
    bs, dec, _ = x_sample.shape
    depth = norm_mix.shape[0]
    past = cache_sb_k.shape[2]
    sb_heads, sb_dh = cache_sb_k.shape[3], cache_sb_k.shape[4]
    band_heads, band_dh = cache_band_k.shape[3], cache_band_k.shape[4]
    band_rows = cache_band_k.shape[2]
    mla_heads = mla_w_uk.shape[2]
    lora = mla_w_uk.shape[1]
    mla_scale = (MLA_NOPE + MLA_ROPE) ** -0.5

    tm_p, tm_s = 256, bs * dec
    tm_ffn = 512
    blk = 256
    fc = 256

    xp = x_prompt.reshape(bp * seq, d)
    xs = x_sample.reshape(bs * dec, d)
    outs = {name: [] for name in ("sb_kp", "sb_vp", "sb_ks", "sb_vs", "band_kp", "band_vp", "band_ks", "band_vs",
                                  "mla_cp", "mla_rp", "mla_cs", "mla_rs")}

    for layer in range(depth):
        kind, j = layer % 3, layer // 3
        g_mix = norm_mix[layer]
        if kind == 0:
            w_bf = sb_w_qkv[j].astype(BF16)
            scale = sb_dh ** -0.5
            q, k, v, kb, vb = _qkv_proj(xp, g_mix, w_bf, scale, tm_p)
            ap = _sb_attention(q.reshape(bp, seq, d), kb.reshape(bp, seq, d), vb.reshape(bp, seq, d),
                               sb_heads, blk, blk, 0)
            outs["sb_kp"].append(k.reshape(bp, seq, sb_heads, sb_dh))
            outs["sb_vp"].append(v.reshape(bp, seq, sb_heads, sb_dh))
            q, k, v, kb, vb = _qkv_proj(xs, g_mix, w_bf, scale, tm_s)
            t_k = -(-(past + dec) // blk) * blk
            k_all = _pad_rows(jnp.concatenate([cache_sb_k[j].reshape(bs, past, d).astype(BF16),
                                               kb.reshape(bs, dec, d)], axis=1), t_k)
            v_all = _pad_rows(jnp.concatenate([cache_sb_v[j].reshape(bs, past, d).astype(BF16),
                                               vb.reshape(bs, dec, d)], axis=1), t_k)
            a_s = _sb_attention(q.reshape(bs, dec, d), k_all, v_all, sb_heads, dec, blk, past)
            outs["sb_ks"].append(k.reshape(bs, dec, sb_heads, sb_dh))
            outs["sb_vs"].append(v.reshape(bs, dec, sb_heads, sb_dh))
            w_o = sb_w_o[j]
        elif kind == 1:
            w_bf = band_w_qkv[j].astype(BF16)
            scale = band_dh ** -0.5
            win = BAND_CHUNKS * CHUNK
            q, k, v, kb, vb = _qkv_proj(xp, g_mix, w_bf, scale, tm_p)
            tile = _band_tile(band_rel_bias[j], win + jnp.arange(blk), jnp.arange(3 * blk),
                              jnp.ones((3 * blk,), bool))
            ap = _band_attention(q.reshape(bp, seq, d), kb.reshape(bp, seq, d), vb.reshape(bp, seq, d), tile,
                                 band_heads, blk, blk, True)
            keep = min(win, seq)
            outs["band_kp"].append(k.reshape(bp, seq, band_heads, band_dh)[:, seq - keep:])
            outs["band_vp"].append(v.reshape(bp, seq, band_heads, band_dh)[:, seq - keep:])
            q, k, v, kb, vb = _qkv_proj(xs, g_mix, w_bf, scale, tm_s)
            n_keys = band_rows + dec
            k_all = _pad_rows(jnp.concatenate([cache_band_k[j].reshape(bs, band_rows, d).astype(BF16),
                                               kb.reshape(bs, dec, d)], axis=1), 3 * blk)
            v_all = _pad_rows(jnp.concatenate([cache_band_v[j].reshape(bs, band_rows, d).astype(BF16),
                                               vb.reshape(bs, dec, d)], axis=1), 3 * blk)
            q_pos = past + jnp.arange(dec)
            k_pos = past - band_rows + jnp.arange(3 * blk)
            tile = _band_tile(band_rel_bias[j], q_pos, k_pos, jnp.arange(3 * blk) < n_keys)
            a_s = _band_attention(q.reshape(bs, dec, d), k_all, v_all, tile, band_heads, dec, blk, False)
            outs["band_ks"].append(k.reshape(bs, dec, band_heads, band_dh))
            outs["band_vs"].append(v.reshape(bs, dec, band_heads, band_dh))
            w_o = band_w_o[j]
        else:
            wts = _mla_weights(mla_w_dq[j], mla_q_norm[j], mla_w_uq[j], mla_w_dkv[j], mla_kv_norm[j],
                               mla_w_uk[j], mla_w_uv[j], mla_heads)
            cos_p, sin_p = _rope_tables(jnp.arange(seq))
            qc, kc, vv, ckv, kr = _mla_proj(xp, g_mix, wts, cos_p, sin_p, mla_heads, mla_scale, tm_p)
            ap = _mla_attention(qc.reshape(bp, seq, -1), kc.reshape(bp, seq, -1), vv.reshape(bp, seq, -1),
                                mla_heads, blk, blk, 0, seq)
            outs["mla_cp"].append(ckv.reshape(bp, seq, lora))
            outs["mla_rp"].append(kr.reshape(bp, seq, MLA_ROPE))
            cos_s, sin_s = _rope_tables(jnp.tile(past + jnp.arange(dec), bs))
            qc, kc, vv, ckv, kr = _mla_proj(xs, g_mix, wts, cos_s, sin_s, mla_heads, mla_scale, tm_s)
            kc_c, v_c = _mla_expand(cache_mla_ckv[j].reshape(bs * past, lora),
                                    cache_mla_krope[j].reshape(bs * past, MLA_ROPE), wts[6], wts[7], mla_heads, blk)
            t_k = -(-(past + dec) // blk) * blk
            kc_all = _pad_rows(jnp.concatenate([kc_c.reshape(bs, past, -1), kc.reshape(bs, dec, -1)], axis=1), t_k)
            v_all = _pad_rows(jnp.concatenate([v_c.reshape(bs, past, -1), vv.reshape(bs, dec, -1)], axis=1), t_k)
            a_s = _mla_attention(qc.reshape(bs, dec, -1), kc_all, v_all, mla_heads, dec, blk, past, past + dec)
            outs["mla_cs"].append(ckv.reshape(bs, dec, lora))
            outs["mla_rs"].append(kr.reshape(bs, dec, MLA_ROPE))
            w_o = mla_w_o[j]

        wgu, wd = _ffn_weights(w_gate_up[layer], w_down[layer], fc)
        final = layer == depth - 1
        wo_bf = w_o.astype(BF16)
        xp = _oproj_ffn(xp, ap.reshape(bp * seq, d), wo_bf, norm_ffn[layer], wgu, wd, norm_final, final, tm_ffn)
        xs = _oproj_ffn(xs, a_s.reshape(bs * dec, d), wo_bf, norm_ffn[layer], wgu, wd, norm_final, final, tm_s)

    st = lambda name: jnp.stack(outs[name])
    return (xp.reshape(bp, seq, d), xs.reshape(bs, dec, d),
            st("sb_kp"), st("sb_vp"), st("band_kp"), st("band_vp"), st("mla_cp"), st("mla_rp"),
            st("sb_ks"), st("sb_vs"), st("band_ks"), st("band_vs"), st("mla_cs"), st("mla_rs"))
```

```python
import functools

import jax
import jax.numpy as jnp
from jax import lax
from jax.experimental import pallas as pl
from jax.experimental.pallas import tpu as pltpu

CHUNK = 64
BAND_CHUNKS = 8
REL_MAX = 128
MLA_NOPE = 128
MLA_ROPE = 64
MLA_V = 128
ROPE_THETA = 10000.0
EPS = 1e-6

LOG2E = 1.4426950408889634
SB_LOG2_ZERO = -151.0

VMEM_LIMIT = 56 * 1024 * 1024
F32 = jnp.float32
BF16 = jnp.bfloat16


def _cparams(n_grid):
    return pltpu.CompilerParams(dimension_semantics=("arbitrary",) * n_grid, vmem_limit_bytes=VMEM_LIMIT)


def _rms(x, g):
    return x * lax.rsqrt(jnp.mean(x * x, axis=-1, keepdims=True) + EPS) * g


def _dot(a, b):
    return jnp.dot(a, b, preferred_element_type=F32)


def _dot_nt(a, b):
    return lax.dot_general(a, b, (((1,), (1,)), ((), ())), preferred_element_type=F32)


def _lane_tiles(x, op):
    tiles = [x[:, t:t + 128] for t in range(0, x.shape[1], 128)]
    while len(tiles) > 1:
        tiles = [op(tiles[t], tiles[t + 1]) for t in range(0, len(tiles) - 1, 2)] + tiles[len(tiles) & ~1:]
    return tiles[0]


def _const_spec(shape):
    nd = len(shape)
    return pl.BlockSpec(shape, lambda *_: (0,) * nd)


def _qkv_kernel(x_ref, g_ref, w_ref, q_ref, k_ref, v_ref, kb_ref, vb_ref, *, d, q_scale, ones_v):
    h = _rms(x_ref[...], g_ref[...]).astype(BF16)
    q = _dot(h, w_ref[:, 0:d])
    q_ref[...] = (q * q_scale).astype(BF16)
    k = _dot(h, w_ref[:, d:2 * d])
    k_ref[...] = k
    kb_ref[...] = k.astype(BF16)
    v = _dot(h, w_ref[:, 2 * d:3 * d])
    v_ref[...] = v
    if ones_v:
        ones = jnp.ones((k.shape[0], 128), BF16)
        for t in range(d // 128):
            vb_ref[:, 256 * t:256 * t + 128] = v[:, 128 * t:128 * (t + 1)].astype(BF16)
            vb_ref[:, 256 * t + 128:256 * (t + 1)] = ones
    else:
        vb_ref[...] = v.astype(BF16)


def _qkv_proj(x2d, g, w_bf, q_scale, tm, ones_v=False):
    n, d = x2d.shape
    row = lambda i: (i, 0)
    wv = 2 * d if ones_v else d
    return pl.pallas_call(
        functools.partial(_qkv_kernel, d=d, q_scale=q_scale, ones_v=ones_v),
        grid=(n // tm,),
        in_specs=[pl.BlockSpec((tm, d), row), _const_spec((1, d)), _const_spec((d, 3 * d))],
        out_specs=[pl.BlockSpec((tm, d), row)] * 4 + [pl.BlockSpec((tm, wv), row)],
        out_shape=[jax.ShapeDtypeStruct((n, d), BF16), jax.ShapeDtypeStruct((n, d), F32),
                   jax.ShapeDtypeStruct((n, d), F32), jax.ShapeDtypeStruct((n, d), BF16),
                   jax.ShapeDtypeStruct((n, wv), BF16)],
        compiler_params=_cparams(1),
        name="qkv_proj",
    )(x2d, g.reshape(1, d), w_bf)


def _ffn_kernel(x_ref, a_ref, wo_ref, g_ref, wgu_ref, wd_ref, gf_ref, o_ref, acc_ref, h_ref, *, n_chunks, fc, final):
    x1 = x_ref[...] + _dot(a_ref[...], wo_ref[...])
    h_ref[...] = _rms(x1, g_ref[...]).astype(BF16)
    acc_ref[...] = x1

    def body(c, carry):
        lo = pl.multiple_of(c * fc, fc)
        gate = _dot(h_ref[...], wgu_ref[:, pl.ds(lo, fc)])
        up = _dot(h_ref[...], wgu_ref[:, pl.ds(pl.multiple_of(n_chunks * fc + lo, fc), fc)])
        act = (gate / (1.0 + jnp.exp(-gate)) * up).astype(BF16)
        acc_ref[...] += _dot(act, wd_ref[pl.ds(lo, fc), :])
        return carry

    lax.fori_loop(0, n_chunks, body, 0)
    out = acc_ref[...]
    if final:
        out = _rms(out, gf_ref[...])
    o_ref[...] = out


def _oproj_ffn(x2d, a2d, wo_bf, g, wgu_c, wd_c, g_final, final, tm, fc):
    n, d = x2d.shape
    n_chunks = wd_c.shape[0] // fc
    row = lambda i: (i, 0)
    return pl.pallas_call(
        functools.partial(_ffn_kernel, n_chunks=n_chunks, fc=fc, final=final),
        grid=(n // tm,),
        in_specs=[pl.BlockSpec((tm, d), row), pl.BlockSpec((tm, d), row), _const_spec((d, d)), _const_spec((1, d)),
                  _const_spec(wgu_c.shape), _const_spec(wd_c.shape), _const_spec((1, d))],
        out_specs=pl.BlockSpec((tm, d), row),
        out_shape=jax.ShapeDtypeStruct((n, d), F32),
        scratch_shapes=[pltpu.VMEM((tm, d), F32), pltpu.VMEM((tm, d), BF16)],
        compiler_params=_cparams(1),
        name="oproj_ffn",
    )(x2d, a2d, wo_bf, g.reshape(1, d), wgu_c, wd_c, g_final.reshape(1, d))


def _sb_kernel(q_ref, k_ref, v_ref, tri_ref, o_ref, a_ref, acc_ref, *, tq, tk, dh, n_heads, q_off):
    i = pl.program_id(2)
    q_start = q_off + i * tq
    q_pos = q_start + lax.broadcasted_iota(jnp.int32, (tq, 1), 0)
    col = lax.broadcasted_iota(jnp.int32, (1, tk), 1)
    j_last = (q_start + tq - 2) // tk
    a_ref[...] = jnp.zeros(a_ref.shape, F32)
    acc_ref[...] = jnp.zeros(acc_ref.shape, F32)

    def cond(state):
        j, a_max = state
        return jnp.logical_and(j >= 0, a_max > SB_LOG2_ZERO)

    def body(state):
        j, _ = state
        start = pl.multiple_of(j * tk, tk)
        mask = (start + col) < q_pos
        tri = tri_ref[...]
        lanes = [pl.ds(c * dh, dh) for c in range(n_heads)]
        zs = [jnp.where(mask, _dot_nt(q_ref[:, ln], k_ref[pl.ds(start, tk), ln]), -jnp.inf) for ln in lanes]
        softs = [jnp.maximum(z, 0.0) + jnp.log(1.0 + jnp.exp2(-jnp.abs(z))) * LOG2E for z in zs]
        rs = []
        for soft in softs:
            hi = soft.astype(BF16)
            lo = (soft - hi.astype(F32)).astype(BF16)
            rs.append(_dot(hi, tri) + _dot(lo, tri))
        a_max = jnp.float32(-jnp.inf)
        ws = []
        for c in range(n_heads):
            a = a_ref[c]
            a_wide = jnp.concatenate([a] * (tk // 128), axis=1)
            ws.append(jnp.exp2(zs[c] + rs[c] + a_wide).astype(BF16))
            a = a - jnp.sum(_lane_tiles(softs[c], jnp.add), axis=-1, keepdims=True)
            a_ref[c] = a
            a_max = jnp.maximum(a_max, jnp.max(a))
        for c in range(n_heads):
            acc_ref[:, lanes[c]] += _dot(ws[c], v_ref[pl.ds(start, tk), lanes[c]])
        return j - 1, a_max

    lax.while_loop(cond, body, (j_last, jnp.float32(0.0)))
    o_ref[...] = acc_ref[...].astype(o_ref.dtype)


def _sb_attention(q, k, v, heads, tq, tk, q_off):
    b, t_q, d = q.shape
    t_k = k.shape[1]
    dh = d // heads
    n_heads = 2
    tri = -jnp.tril(jnp.ones((tk, tk), F32)).astype(BF16)
    return pl.pallas_call(
        functools.partial(_sb_kernel, tq=tq, tk=tk, dh=dh, n_heads=n_heads, q_off=q_off),
        grid=(b, heads // n_heads, t_q // tq),
        in_specs=[pl.BlockSpec((None, tq, n_heads * dh), lambda b_, h, i: (b_, i, h)),
                  pl.BlockSpec((None, t_k, n_heads * dh), lambda b_, h, i: (b_, 0, h)),
                  pl.BlockSpec((None, t_k, n_heads * dh), lambda b_, h, i: (b_, 0, h)),
                  _const_spec((tk, tk))],
        out_specs=pl.BlockSpec((None, tq, n_heads * dh), lambda b_, h, i: (b_, i, h)),
        out_shape=jax.ShapeDtypeStruct((b, t_q, d), BF16),
        scratch_shapes=[pltpu.VMEM((n_heads, tq, 128), F32), pltpu.VMEM((tq, n_heads * dh), F32)],
        compiler_params=_cparams(3),
        name="sb_attention",
    )(q, k, v, tri)


def _band_kernel(q_ref, k0_ref, k1_ref, k2_ref, v0_ref, v1_ref, v2_ref, tile_ref, o_ref, *, tk, prompt):
    q = q_ref[...]
    tq, pair = q.shape
    kw = jnp.concatenate([k0_ref[...], k1_ref[...], k2_ref[...]], axis=0)
    vw = jnp.concatenate([v0_ref[...], v1_ref[...], v2_ref[...]], axis=0)
    half = pair // 2
    lane = lax.broadcasted_iota(jnp.int32, (1, pair), 1)
    col = lax.broadcasted_iota(jnp.int32, (1, 3 * tk), 1)

    def compute(first_col):
        heads = [(lane >= hh * half) & (lane < (hh + 1) * half) for hh in range(2)]
        scores = [_dot_nt(jnp.where(in_head, q, jnp.zeros_like(q)), kw) + tile_ref[hh]
                  for hh, in_head in enumerate(heads)]
        if first_col is not None:
            scores = [jnp.where(col >= first_col, s, -jnp.inf) for s in scores]
        weights = []
        for s in scores:
            m = jnp.broadcast_to(jnp.max(_lane_tiles(s, jnp.maximum), axis=-1, keepdims=True), (tq, 128))
            weights.append(jnp.exp2(s - jnp.concatenate([m] * (3 * tk // 128), axis=1)).astype(BF16))
        outs = []
        for p in weights:
            r = _dot(p, vw)
            outs.append(r[:, :pair] / r[:, pair:])
        o_ref[...] = jnp.where(lane < half, outs[0], outs[1]).astype(o_ref.dtype)

    if prompt:
        i = pl.program_id(2)

        @pl.when(i < 2)
        def _():
            compute((2 - i) * tk)

        @pl.when(i >= 2)
        def _():
            compute(None)
    else:
        compute(None)


def _band_tile(rel_bias, q0, n_q, k0, n_k, n_valid):
    heads = rel_bias.shape[0]
    n = n_q + n_k - 1
    dist = (q0 - k0) + (n_q - 1) - jnp.arange(n)
    g = rel_bias[:, jnp.clip(dist, -REL_MAX, REL_MAX) + REL_MAX].astype(F32)
    flat = jnp.tile(jnp.pad(g, ((0, 0), (0, 1))), (1, n_q))[:, :n_q * n]
    bias = flat.reshape(heads, n_q, n)[:, :, n_q - 1:]
    q_chunk = ((q0 + jnp.arange(n_q)) // CHUNK)[:, None]
    k_chunk = ((k0 + jnp.arange(n_k)) // CHUNK)[None, :]
    mask = (jnp.arange(n_k) < n_valid)[None, :] & (k_chunk <= q_chunk) & (k_chunk >= q_chunk - BAND_CHUNKS)
    return jnp.where(mask[None], bias, -jnp.inf)


def _band_attention(q, k, v, tile, heads, tq, tk, prompt):
    b, t_q, d = q.shape
    pair = 2 * (d // heads)
    if prompt:
        kmap = lambda t: (lambda hp, b_, i: (b_, jnp.maximum(i - 2 + t, 0), hp))
    else:
        kmap = lambda t: (lambda hp, b_, i: (b_, t, hp))
    kspecs = [pl.BlockSpec((None, tk, pair), kmap(t)) for t in range(3)]
    vspecs = [pl.BlockSpec((None, tk, 2 * pair), kmap(t)) for t in range(3)]
    return pl.pallas_call(
        functools.partial(_band_kernel, tk=tk, prompt=prompt),
        grid=(heads // 2, b, t_q // tq),
        in_specs=[pl.BlockSpec((None, tq, pair), lambda hp, b_, i: (b_, i, hp))] + kspecs + vspecs
        + [pl.BlockSpec((2, tq, 3 * tk), lambda hp, b_, i: (hp, 0, 0))],
        out_specs=pl.BlockSpec((None, tq, pair), lambda hp, b_, i: (b_, i, hp)),
        out_shape=jax.ShapeDtypeStruct((b, t_q, d), BF16),
        compiler_params=_cparams(3),
        name="band_attention",
    )(q, k, k, k, v, v, v, tile)


def _mla_proj_kernel(x_ref, g_ref, wdq_ref, qn_ref, wqc_ref, wqs_ref, wdkv_ref, kvn_ref, wuk_ref, wuv_ref,
                     cos_ref, sin_ref, qc_ref, kc_ref, v_ref, ckv_ref, kr_ref, *, heads, scale):
    w = MLA_NOPE
    h = _rms(x_ref[...], g_ref[...]).astype(BF16)
    qn = _rms(_dot(h, wdq_ref[...]), qn_ref[...]).astype(BF16)
    a = _dot(qn, wqc_ref[...])
    sw = _dot(qn, wqs_ref[...])
    cos = cos_ref[...]
    sin = sin_ref[...]
    for hd in range(heads):
        qc_ref[:, 2 * w * hd:2 * w * hd + w] = (a[:, 2 * w * hd:2 * w * hd + w] * scale).astype(BF16)
        rope = a[:, 2 * w * hd + w:2 * w * (hd + 1)] * cos + sw[:, w * hd:w * (hd + 1)] * sin
        qc_ref[:, 2 * w * hd + w:2 * w * (hd + 1)] = (rope * scale).astype(BF16)
    kv = _dot(h, wdkv_ref[...])
    lora = kv.shape[1] - 2 * w
    ckv = _rms(kv[:, :lora], kvn_ref[...])
    ckv_ref[...] = ckv
    kr = kv[:, lora:lora + w] * cos + kv[:, lora + w:lora + 2 * w] * sin
    kr_ref[...] = kr[:, :MLA_ROPE]
    cb = ckv.astype(BF16)
    kn = _dot(cb, wuk_ref[...])
    vv = _dot(cb, wuv_ref[...])
    krb = kr.astype(BF16)
    ones = jnp.ones((krb.shape[0], w), BF16)
    for hd in range(heads):
        kc_ref[:, 2 * w * hd:2 * w * hd + w] = kn[:, w * hd:w * (hd + 1)].astype(BF16)
        kc_ref[:, 2 * w * hd + w:2 * w * (hd + 1)] = krb
        v_ref[:, 2 * w * hd:2 * w * hd + w] = vv[:, w * hd:w * (hd + 1)].astype(BF16)
        v_ref[:, 2 * w * hd + w:2 * w * (hd + 1)] = ones


def _mla_proj(x2d, g, wts, cos_t, sin_t, heads, scale, tm):
    n, d = x2d.shape
    wdq, qn, wqc, wqs, wdkv, kvn, wuk, wuv = wts
    lora = wuk.shape[0]
    n_pos = cos_t.shape[0] // tm
    row = lambda i: (i, 0)
    pos = lambda i: (i % n_pos, 0)
    return pl.pallas_call(
        functools.partial(_mla_proj_kernel, heads=heads, scale=scale),
        grid=(n // tm,),
        in_specs=[pl.BlockSpec((tm, d), row), _const_spec((1, d)), _const_spec(wdq.shape), _const_spec(qn.shape),
                  _const_spec(wqc.shape), _const_spec(wqs.shape), _const_spec(wdkv.shape), _const_spec(kvn.shape),
                  _const_spec(wuk.shape), _const_spec(wuv.shape),
                  pl.BlockSpec((tm, MLA_NOPE), pos), pl.BlockSpec((tm, MLA_NOPE), pos)],
        out_specs=[pl.BlockSpec((tm, heads * 2 * MLA_NOPE), row), pl.BlockSpec((tm, heads * 2 * MLA_NOPE), row),
                   pl.BlockSpec((tm, heads * 2 * MLA_V), row), pl.BlockSpec((tm, lora), row),
                   pl.BlockSpec((tm, MLA_ROPE), row)],
        out_shape=[jax.ShapeDtypeStruct((n, heads * 2 * MLA_NOPE), BF16),
                   jax.ShapeDtypeStruct((n, heads * 2 * MLA_NOPE), BF16),
                   jax.ShapeDtypeStruct((n, heads * 2 * MLA_V), BF16),
                   jax.ShapeDtypeStruct((n, lora), F32),
                   jax.ShapeDtypeStruct((n, MLA_ROPE), F32)],
        compiler_params=_cparams(1),
        name="mla_proj",
    )(x2d, g.reshape(1, d), wdq, qn, wqc, wqs, wdkv, kvn, wuk, wuv, cos_t, sin_t)


def _mla_expand_kernel(c_ref, r_ref, wuk_ref, wuv_ref, kc_ref, v_ref, *, heads):
    w = MLA_NOPE
    cb = c_ref[...].astype(BF16)
    kn = _dot(cb, wuk_ref[...])
    vv = _dot(cb, wuv_ref[...])
    krb = r_ref[...].astype(BF16)
    zeros = jnp.zeros((krb.shape[0], w - MLA_ROPE), BF16)
    ones = jnp.ones((krb.shape[0], w), BF16)
    for hd in range(heads):
        kc_ref[:, 2 * w * hd:2 * w * hd + w] = kn[:, w * hd:w * (hd + 1)].astype(BF16)
        kc_ref[:, 2 * w * hd + w:2 * w * hd + w + MLA_ROPE] = krb
        kc_ref[:, 2 * w * hd + w + MLA_ROPE:2 * w * (hd + 1)] = zeros
        v_ref[:, 2 * w * hd:2 * w * hd + w] = vv[:, w * hd:w * (hd + 1)].astype(BF16)
        v_ref[:, 2 * w * hd + w:2 * w * (hd + 1)] = ones


def _mla_expand(c2d, r2d, wuk, wuv, heads, tm):
    n, lora = c2d.shape
    row = lambda i: (i, 0)
    return pl.pallas_call(
        functools.partial(_mla_expand_kernel, heads=heads),
        grid=(n // tm,),
        in_specs=[pl.BlockSpec((tm, lora), row), pl.BlockSpec((tm, MLA_ROPE), row),
                  _const_spec(wuk.shape), _const_spec(wuv.shape)],
        out_specs=[pl.BlockSpec((tm, heads * 2 * MLA_NOPE), row), pl.BlockSpec((tm, heads * 2 * MLA_V), row)],
        out_shape=[jax.ShapeDtypeStruct((n, heads * 2 * MLA_NOPE), BF16),
                   jax.ShapeDtypeStruct((n, heads * 2 * MLA_V), BF16)],
        compiler_params=_cparams(1),
        name="mla_expand",
    )(c2d, r2d, wuk, wuv)


def _mla_attn_kernel(q_ref, kt_ref, v_ref, o_ref, m_ref, alpha_ref, p_ref, acc_ref, *, tq, tk, group, q_off, kv_len,
                     n_mask, mask_all):
    i = pl.program_id(2)
    q_start = q_off + i * tq
    q_chunk = (q_start + lax.broadcasted_iota(jnp.int32, (tq, 1), 0)) // CHUNK
    col = lax.broadcasted_iota(jnp.int32, (1, tk), 1)
    k_end = jnp.minimum(((q_start + tq - 1) // CHUNK + 1) * CHUNK, kv_len)
    n_blk = (k_end + tk - 1) // tk
    m_ref[...] = jnp.full(m_ref.shape, -jnp.inf, F32)
    acc_ref[...] = jnp.zeros(acc_ref.shape, F32)
    strip = min(tq, 64)

    n_rows = 2 if tq >= 256 else 1
    rh = tq // n_rows

    def process(blocks):
        starts = [pl.multiple_of(j * tk, tk) for j, _ in blocks]
        scores = [[_dot(q_ref[pl.ds(h * rh, rh), :], kt_ref[:, pl.ds(st, tk)]) for h in range(n_rows)]
                  for st in starts]
        for n, ((_, masked), st, s_halves) in enumerate(zip(blocks, starts, scores)):
            c = n % 2
            for r0 in range(0, tq, strip):
                rows = pl.ds(r0, strip)
                s = s_halves[r0 // rh][r0 % rh:r0 % rh + strip, :]
                if masked:
                    k_pos = st + col
                    s = jnp.where(((k_pos // CHUNK) <= q_chunk[r0:r0 + strip]) & (k_pos < kv_len), s, -jnp.inf)
                m_prev = m_ref[c, rows, :]
                m_new = jnp.maximum(m_prev, jnp.max(_lane_tiles(s, jnp.maximum), axis=-1, keepdims=True))
                p_ref[n, rows, :] = jnp.exp2(s - jnp.concatenate([m_new] * (tk // 128), axis=1)).astype(BF16)
                alpha_ref[n, rows, :] = jnp.exp2(m_prev - m_new)
                m_ref[c, rows, :] = m_new
        for n, st in enumerate(starts):
            c = n % 2
            for h in range(n_rows):
                rows = pl.ds(h * rh, rh)
                alpha = alpha_ref[n, rows, :]
                acc_ref[c, rows, :] = (jnp.concatenate([alpha, alpha], axis=1) * acc_ref[c, rows, :]
                                       + _dot(p_ref[n, rows, :], v_ref[pl.ds(st, tk), :]))

    def body(g, carry):
        process([(group * g + n, mask_all) for n in range(group)])
        return carry

    n_old = n_blk - n_mask
    n_groups = n_old // group
    lax.fori_loop(0, n_groups, body, 0)
    first = group * n_groups
    for rest in range(group):
        @pl.when(n_old - first == rest)
        def _(rest=rest):
            process([(first + n, mask_all) for n in range(rest)]
                    + [(first + rest + n, True) for n in range(n_mask)])

    m0, m1 = m_ref[0], m_ref[1]
    m = jnp.maximum(m0, m1)
    a0 = jnp.exp2(m0 - m)
    a1 = jnp.exp2(m1 - m)
    acc = jnp.concatenate([a0, a0], axis=1) * acc_ref[0] + jnp.concatenate([a1, a1], axis=1) * acc_ref[1]
    o_ref[...] = (acc[:, :MLA_V] / acc[:, MLA_V:]).astype(o_ref.dtype)


def _mla_attention(qc, kc, v1, heads, tq, tk, q_off, kv_len):
    b, t_q, _ = qc.shape
    t_k = kc.shape[1]
    w = 2 * MLA_NOPE
    mask_all = not (q_off == 0 and tq % tk == 0 and t_q == t_k == kv_len)
    n_mask = 1 if mask_all else tq // tk
    group = 4
    slots = group - 1 + n_mask
    kt = jnp.transpose(kc, (0, 2, 1))
    return pl.pallas_call(
        functools.partial(_mla_attn_kernel, tq=tq, tk=tk, group=group, q_off=q_off, kv_len=kv_len,
                          n_mask=n_mask, mask_all=mask_all),
        grid=(b, heads, t_q // tq),
        in_specs=[pl.BlockSpec((None, tq, w), lambda b_, h, i: (b_, i, h)),
                  pl.BlockSpec((None, w, t_k), lambda b_, h, i: (b_, h, 0)),
                  pl.BlockSpec((None, t_k, MLA_V + 128), lambda b_, h, i: (b_, 0, h))],
        out_specs=pl.BlockSpec((None, tq, MLA_V), lambda b_, h, i: (b_, i, h)),
        out_shape=jax.ShapeDtypeStruct((b, t_q, heads * MLA_V), BF16),
        scratch_shapes=[pltpu.VMEM((2, tq, 128), F32), pltpu.VMEM((slots, tq, 128), F32),
                        pltpu.VMEM((slots, tq, tk), BF16),
                        pltpu.VMEM((2, tq, MLA_V + 128), F32)],
        compiler_params=_cparams(3),
        name="mla_attention",
    )(qc, kt, v1)


def _rope_tables(pos):
    half = MLA_ROPE // 2
    inv_freq = ROPE_THETA ** (-jnp.arange(half, dtype=F32) / half)
    ang = pos.astype(F32)[:, None] * inv_freq[None, :]
    zeros = jnp.zeros((pos.shape[0], MLA_NOPE - MLA_ROPE), F32)
    cos, sin = jnp.cos(ang), jnp.sin(ang)
    return jnp.concatenate([cos, cos, zeros], axis=1), jnp.concatenate([sin, sin, zeros], axis=1)


def _swap_halves(w):
    half = w.shape[-1] // 2
    return jnp.concatenate([-w[..., half:], w[..., :half]], axis=-1)


def _mla_weights(w_dq, q_norm, w_uq, w_dkv, kv_norm, w_uk, w_uv, heads):
    q_lora = w_dq.shape[1]
    lora = w_uk.shape[0]
    pad = MLA_NOPE - MLA_ROPE
    uq = w_uq.reshape(q_lora, heads, MLA_NOPE + MLA_ROPE)
    zq = jnp.zeros((q_lora, heads, pad), F32)
    wqc = jnp.concatenate([uq, zq], axis=-1).reshape(q_lora, heads * 2 * MLA_NOPE)
    wqs = jnp.concatenate([_swap_halves(uq[..., MLA_NOPE:]), zq], axis=-1).reshape(q_lora, heads * MLA_NOPE)
    zk = jnp.zeros((w_dkv.shape[0], pad), F32)
    rope = w_dkv[:, lora:]
    wdkv = jnp.concatenate([w_dkv[:, :lora], rope, zk, _swap_halves(rope), zk], axis=1)
    return (w_dq.astype(BF16), q_norm.reshape(1, q_lora), wqc.astype(BF16), wqs.astype(BF16), wdkv.astype(BF16),
            kv_norm.reshape(1, lora), w_uk.reshape(lora, heads * MLA_NOPE).astype(BF16),
            w_uv.reshape(lora, heads * MLA_V).astype(BF16))


def _pad_rows(x, rows):
    return jnp.pad(x, ((0, 0), (0, rows - x.shape[1]), (0, 0)))


def kernel(x_prompt, x_sample, cache_sb_k, cache_sb_v, cache_band_k, cache_band_v, cache_mla_ckv, cache_mla_krope, norm_mix, norm_ffn, norm_final, w_gate_up, w_down, sb_w_qkv, sb_w_o, band_w_qkv, band_rel_bias, band_w_o, mla_w_dq, mla_q_norm, mla_w_uq, mla_w_dkv, mla_kv_norm, mla_w_uk, mla_w_uv, mla_w_o):
    bp, seq, d = x_prompt.shape
    bs, dec, _ = x_sample.shape
    depth = norm_mix.shape[0]
    past = cache_sb_k.shape[2]
    sb_heads, sb_dh = cache_sb_k.shape[3], cache_sb_k.shape[4]
    band_heads, band_dh = cache_band_k.shape[3], cache_band_k.shape[4]
    band_rows = cache_band_k.shape[2]
    mla_heads = mla_w_uk.shape[2]
    lora = mla_w_uk.shape[1]
    mla_scale = (MLA_NOPE + MLA_ROPE) ** -0.5 * LOG2E

    tm_p, tm_s = 512, bs * dec
    tm_ffn = 512
    blk = 256
    mla_blk = 512
    fc = 256

    xp = x_prompt.reshape(bp * seq, d)
    xs = x_sample.reshape(bs * dec, d)
    outs = {name: [] for name in ("sb_kp", "sb_vp", "sb_ks", "sb_vs", "band_kp", "band_vp", "band_ks", "band_vs",
                                  "mla_cp", "mla_rp", "mla_cs", "mla_rs")}

    for layer in range(depth):
        kind, j = layer % 3, layer // 3
        g_mix = norm_mix[layer]
        if kind == 0:
            w_bf = sb_w_qkv[j].astype(BF16)
            scale = sb_dh ** -0.5 * LOG2E
            q, k, v, kb, vb = _qkv_proj(xp, g_mix, w_bf, scale, tm_p)
            ap = _sb_attention(q.reshape(bp, seq, d), kb.reshape(bp, seq, d), vb.reshape(bp, seq, d),
                               sb_heads, blk, blk, 0)
            outs["sb_kp"].append(k.reshape(bp, seq, sb_heads, sb_dh))
            outs["sb_vp"].append(v.reshape(bp, seq, sb_heads, sb_dh))
            q, k, v, kb, vb = _qkv_proj(xs, g_mix, w_bf, scale, tm_s)
            t_k = -(-(past + dec) // blk) * blk
            k_all = _pad_rows(jnp.concatenate([cache_sb_k[j].reshape(bs, past, d).astype(BF16),
                                               kb.reshape(bs, dec, d)], axis=1), t_k)
            v_all = _pad_rows(jnp.concatenate([cache_sb_v[j].reshape(bs, past, d).astype(BF16),
                                               vb.reshape(bs, dec, d)], axis=1), t_k)
            a_s = _sb_attention(q.reshape(bs, dec, d), k_all, v_all, sb_heads, dec, blk, past)
            outs["sb_ks"].append(k.reshape(bs, dec, sb_heads, sb_dh))
            outs["sb_vs"].append(v.reshape(bs, dec, sb_heads, sb_dh))
            w_o = sb_w_o[j]
        elif kind == 1:
            w_bf = band_w_qkv[j].astype(BF16)
            scale = band_dh ** -0.5 * LOG2E
            win = BAND_CHUNKS * CHUNK
            q, k, v, kb, vb = _qkv_proj(xp, g_mix, w_bf, scale, tm_p, ones_v=True)
            tile = _band_tile(band_rel_bias[j], win, blk, 0, 3 * blk, 3 * blk) * LOG2E
            ap = _band_attention(q.reshape(bp, seq, d), kb.reshape(bp, seq, d), vb.reshape(bp, seq, 2 * d), tile,
                                 band_heads, blk, blk, True)
            keep = min(win, seq)
            outs["band_kp"].append(k.reshape(bp, seq, d)[:, seq - keep:].reshape(bp, keep, band_heads, band_dh))
            outs["band_vp"].append(v.reshape(bp, seq, d)[:, seq - keep:].reshape(bp, keep, band_heads, band_dh))
            q, k, v, kb, vb = _qkv_proj(xs, g_mix, w_bf, scale, tm_s, ones_v=True)
            n_keys = band_rows + dec
            k_all = _pad_rows(jnp.concatenate([cache_band_k[j].reshape(bs, band_rows, d).astype(BF16),
                                               kb.reshape(bs, dec, d)], axis=1), 3 * blk)
            cache_v = cache_band_v[j].reshape(bs, band_rows, d // 128, 128).astype(BF16)
            cache_v = jnp.concatenate([cache_v, jnp.ones_like(cache_v)], axis=-1).reshape(bs, band_rows, 2 * d)
            v_all = _pad_rows(jnp.concatenate([cache_v, vb.reshape(bs, dec, 2 * d)], axis=1), 3 * blk)
            tile = _band_tile(band_rel_bias[j], past, dec, past - band_rows, 3 * blk, n_keys) * LOG2E
            a_s = _band_attention(q.reshape(bs, dec, d), k_all, v_all, tile, band_heads, dec, blk, False)
            outs["band_ks"].append(k.reshape(bs, dec, band_heads, band_dh))
            outs["band_vs"].append(v.reshape(bs, dec, band_heads, band_dh))
            w_o = band_w_o[j]
        else:
            wts = _mla_weights(mla_w_dq[j], mla_q_norm[j], mla_w_uq[j], mla_w_dkv[j], mla_kv_norm[j],
                               mla_w_uk[j], mla_w_uv[j], mla_heads)
            cos_p, sin_p = _rope_tables(jnp.arange(seq))
            qc, kc, vv, ckv, kr = _mla_proj(xp, g_mix, wts, cos_p, sin_p, mla_heads, mla_scale, tm_p)
            ap = _mla_attention(qc.reshape(bp, seq, -1), kc.reshape(bp, seq, -1), vv.reshape(bp, seq, -1),
                                mla_heads, mla_blk, mla_blk, 0, seq)
            outs["mla_cp"].append(ckv.reshape(bp, seq, lora))
            outs["mla_rp"].append(kr.reshape(bp, seq, MLA_ROPE))
            cos_s, sin_s = _rope_tables(jnp.tile(past + jnp.arange(dec), bs))
            qc, kc, vv, ckv, kr = _mla_proj(xs, g_mix, wts, cos_s, sin_s, mla_heads, mla_scale, tm_s)
            kc_c, v_c = _mla_expand(cache_mla_ckv[j].reshape(bs * past, lora),
                                    cache_mla_krope[j].reshape(bs * past, MLA_ROPE), wts[6], wts[7], mla_heads, blk)
            t_k = -(-(past + dec) // blk) * blk
            kc_all = _pad_rows(jnp.concatenate([kc_c.reshape(bs, past, -1), kc.reshape(bs, dec, -1)], axis=1), t_k)
            v_all = _pad_rows(jnp.concatenate([v_c.reshape(bs, past, -1), vv.reshape(bs, dec, -1)], axis=1), t_k)
            a_s = _mla_attention(qc.reshape(bs, dec, -1), kc_all, v_all, mla_heads, dec, blk, past, past + dec)
            outs["mla_cs"].append(ckv.reshape(bs, dec, lora))
            outs["mla_rs"].append(kr.reshape(bs, dec, MLA_ROPE))
            w_o = mla_w_o[j]

        wgu, wd = w_gate_up[layer].astype(BF16), w_down[layer].astype(BF16)
        final = layer == depth - 1
        wo_bf = w_o.astype(BF16)
        xp = _oproj_ffn(xp, ap.reshape(bp * seq, d), wo_bf, norm_ffn[layer], wgu, wd, norm_final, final, tm_ffn, fc)
        xs = _oproj_ffn(xs, a_s.reshape(bs * dec, d), wo_bf, norm_ffn[layer], wgu, wd, norm_final, final, tm_s, fc)

    st = lambda name: jnp.stack(outs[name])
    return (xp.reshape(bp, seq, d), xs.reshape(bs, dec, d),
            st("sb_kp"), st("sb_vp"), st("band_kp"), st("band_vp"), st("mla_cp"), st("mla_rp"),
            st("sb_ks"), st("sb_vs"), st("band_ks"), st("band_vs"), st("mla_cs"), st("mla_rs"))
```

```python
import functools

import jax
import jax.numpy as jnp
from jax import lax
from jax.experimental import pallas as pl
from jax.experimental.pallas import tpu as pltpu

CHUNK = 64
BAND_CHUNKS = 8
REL_MAX = 128
MLA_NOPE = 128
MLA_ROPE = 64
MLA_V = 128
ROPE_THETA = 10000.0
EPS = 1e-6

LOG2E = 1.4426950408889634
SB_LOG2_ZERO = -151.0

VMEM_LIMIT = 56 * 1024 * 1024
F32 = jnp.float32
BF16 = jnp.bfloat16


def _cparams(n_grid):
    return pltpu.CompilerParams(dimension_semantics=("arbitrary",) * n_grid, vmem_limit_bytes=VMEM_LIMIT)


def _rms(x, g):
    return x * lax.rsqrt(jnp.mean(x * x, axis=-1, keepdims=True) + EPS) * g


def _dot(a, b):
    return jnp.dot(a, b, preferred_element_type=F32)


def _dot_nt(a, b):
    return lax.dot_general(a, b, (((1,), (1,)), ((), ())), preferred_element_type=F32)


def _lane_tiles(x, op):
    tiles = [x[:, t:t + 128] for t in range(0, x.shape[1], 128)]
    while len(tiles) > 1:
        tiles = [op(tiles[t], tiles[t + 1]) for t in range(0, len(tiles) - 1, 2)] + tiles[len(tiles) & ~1:]
    return tiles[0]


def _const_spec(shape):
    nd = len(shape)
    return pl.BlockSpec(shape, lambda *_: (0,) * nd)


def _qkv_kernel(x_ref, g_ref, w_ref, q_ref, k_ref, v_ref, kb_ref, vb_ref, *, d, q_scale, ones_v):
    h = _rms(x_ref[...], g_ref[...]).astype(BF16)
    q = _dot(h, w_ref[:, 0:d])
    q_ref[...] = (q * q_scale).astype(BF16)
    k = _dot(h, w_ref[:, d:2 * d])
    k_ref[...] = k
    kb_ref[...] = k.astype(BF16)
    v = _dot(h, w_ref[:, 2 * d:3 * d])
    v_ref[...] = v
    if ones_v:
        ones = jnp.ones((k.shape[0], 128), BF16)
        for t in range(d // 128):
            vb_ref[:, 256 * t:256 * t + 128] = v[:, 128 * t:128 * (t + 1)].astype(BF16)
            vb_ref[:, 256 * t + 128:256 * (t + 1)] = ones
    else:
        vb_ref[...] = v.astype(BF16)


def _qkv_proj(x2d, g, w_bf, q_scale, tm, ones_v=False):
    n, d = x2d.shape
    row = lambda i: (i, 0)
    wv = 2 * d if ones_v else d
    return pl.pallas_call(
        functools.partial(_qkv_kernel, d=d, q_scale=q_scale, ones_v=ones_v),
        grid=(n // tm,),
        in_specs=[pl.BlockSpec((tm, d), row), _const_spec((1, d)), _const_spec((d, 3 * d))],
        out_specs=[pl.BlockSpec((tm, d), row)] * 4 + [pl.BlockSpec((tm, wv), row)],
        out_shape=[jax.ShapeDtypeStruct((n, d), BF16), jax.ShapeDtypeStruct((n, d), F32),
                   jax.ShapeDtypeStruct((n, d), F32), jax.ShapeDtypeStruct((n, d), BF16),
                   jax.ShapeDtypeStruct((n, wv), BF16)],
        compiler_params=_cparams(1),
        name="qkv_proj",
    )(x2d, g.reshape(1, d), w_bf)


def _ffn_kernel(x_ref, a_ref, wo_ref, g_ref, wgu_ref, wd_ref, gf_ref, o_ref, acc_ref, h_ref, *, n_chunks, fc, final):
    x1 = x_ref[...] + _dot(a_ref[...], wo_ref[...])
    h_ref[...] = _rms(x1, g_ref[...]).astype(BF16)
    acc_ref[...] = x1

    def body(c, carry):
        lo = pl.multiple_of(c * fc, fc)
        gate = _dot(h_ref[...], wgu_ref[:, pl.ds(lo, fc)])
        up = _dot(h_ref[...], wgu_ref[:, pl.ds(pl.multiple_of(n_chunks * fc + lo, fc), fc)])
        act = (gate / (1.0 + jnp.exp(-gate)) * up).astype(BF16)
        acc_ref[...] += _dot(act, wd_ref[pl.ds(lo, fc), :])
        return carry

    lax.fori_loop(0, n_chunks, body, 0, unroll=True)
    out = acc_ref[...]
    if final:
        out = _rms(out, gf_ref[...])
    o_ref[...] = out


def _oproj_ffn(x2d, a2d, wo_bf, g, wgu_c, wd_c, g_final, final, tm, fc):
    n, d = x2d.shape
    n_chunks = wd_c.shape[0] // fc
    row = lambda i: (i, 0)
    return pl.pallas_call(
        functools.partial(_ffn_kernel, n_chunks=n_chunks, fc=fc, final=final),
        grid=(n // tm,),
        in_specs=[pl.BlockSpec((tm, d), row), pl.BlockSpec((tm, d), row), _const_spec((d, d)), _const_spec((1, d)),
                  _const_spec(wgu_c.shape), _const_spec(wd_c.shape), _const_spec((1, d))],
        out_specs=pl.BlockSpec((tm, d), row),
        out_shape=jax.ShapeDtypeStruct((n, d), F32),
        scratch_shapes=[pltpu.VMEM((tm, d), F32), pltpu.VMEM((tm, d), BF16)],
        compiler_params=_cparams(1),
        name="oproj_ffn",
    )(x2d, a2d, wo_bf, g.reshape(1, d), wgu_c, wd_c, g_final.reshape(1, d))


def _sb_kernel(q_ref, k_ref, v_ref, tri_ref, o_ref, a_ref, acc_ref, *, tq, tk, dh, n_heads, q_off):
    i = pl.program_id(2)
    q_start = q_off + i * tq
    q_pos = q_start + lax.broadcasted_iota(jnp.int32, (tq, 1), 0)
    col = lax.broadcasted_iota(jnp.int32, (1, tk), 1)
    j_last = (q_start + tq - 2) // tk
    a_ref[...] = jnp.zeros(a_ref.shape, F32)
    acc_ref[...] = jnp.zeros(acc_ref.shape, F32)

    def cond(state):
        j, a_max = state
        return jnp.logical_and(j >= 0, a_max > SB_LOG2_ZERO)

    def body(state):
        j, _ = state
        start = pl.multiple_of(j * tk, tk)
        mask = (start + col) < q_pos
        tri = tri_ref[...]
        lanes = [pl.ds(c * dh, dh) for c in range(n_heads)]
        zs = [jnp.where(mask, _dot_nt(q_ref[:, ln], k_ref[pl.ds(start, tk), ln]), -jnp.inf) for ln in lanes]
        softs = [jnp.maximum(z, 0.0) + jnp.log(1.0 + jnp.exp2(-jnp.abs(z))) * LOG2E for z in zs]
        rs = []
        for soft in softs:
            hi = soft.astype(BF16)
            lo = (soft - hi.astype(F32)).astype(BF16)
            rs.append(_dot(hi, tri) + _dot(lo, tri))
        a_max = jnp.float32(-jnp.inf)
        ws = []
        for c in range(n_heads):
            a = a_ref[c]
            a_wide = jnp.concatenate([a] * (tk // 128), axis=1)
            ws.append(jnp.exp2(zs[c] + rs[c] + a_wide).astype(BF16))
            a = a - jnp.sum(_lane_tiles(softs[c], jnp.add), axis=-1, keepdims=True)
            a_ref[c] = a
            a_max = jnp.maximum(a_max, jnp.max(a))
        for c in range(n_heads):
            acc_ref[:, lanes[c]] += _dot(ws[c], v_ref[pl.ds(start, tk), lanes[c]])
        return j - 1, a_max

    lax.while_loop(cond, body, (j_last, jnp.float32(0.0)))
    o_ref[...] = acc_ref[...].astype(o_ref.dtype)


def _sb_keys_kernel(ck_ref, cv_ref, nk_ref, nv_ref, k_ref, v_ref, *, heads, past, dec):
    dh = ck_ref.shape[2]
    for c_ref, n_ref, o_ref in ((ck_ref, nk_ref, k_ref), (cv_ref, nv_ref, v_ref)):
        for hd in range(heads):
            o_ref[0:past, hd * dh:(hd + 1) * dh] = c_ref[:, hd, :].astype(BF16)
        o_ref[past:past + dec, :] = n_ref[...]
        o_ref[past + dec:, :] = jnp.zeros((o_ref.shape[0] - past - dec, o_ref.shape[1]), BF16)


def _sb_keys(cache_k, cache_v, layer, new_k, new_v, t_k):
    _, b, past, heads, dh = cache_k.shape
    dec, d = new_k.shape[1], new_k.shape[2]
    cache_spec = pl.BlockSpec((None, None, past, heads, dh), lambda i: (layer, i, 0, 0, 0))
    new_spec = pl.BlockSpec((None, dec, d), lambda i: (i, 0, 0))
    out_spec = pl.BlockSpec((None, t_k, d), lambda i: (i, 0, 0))
    return pl.pallas_call(
        functools.partial(_sb_keys_kernel, heads=heads, past=past, dec=dec),
        grid=(b,),
        in_specs=[cache_spec, cache_spec, new_spec, new_spec],
        out_specs=[out_spec, out_spec],
        out_shape=[jax.ShapeDtypeStruct((b, t_k, d), BF16)] * 2,
        compiler_params=_cparams(1),
        name="sb_keys",
    )(cache_k, cache_v, new_k, new_v)


def _sb_attention(q, k, v, heads, tq, tk, q_off):
    b, t_q, d = q.shape
    t_k = k.shape[1]
    dh = d // heads
    n_heads = 2
    tri = -jnp.tril(jnp.ones((tk, tk), F32)).astype(BF16)
    return pl.pallas_call(
        functools.partial(_sb_kernel, tq=tq, tk=tk, dh=dh, n_heads=n_heads, q_off=q_off),
        grid=(b, heads // n_heads, t_q // tq),
        in_specs=[pl.BlockSpec((None, tq, n_heads * dh), lambda b_, h, i: (b_, i, h)),
                  pl.BlockSpec((None, t_k, n_heads * dh), lambda b_, h, i: (b_, 0, h)),
                  pl.BlockSpec((None, t_k, n_heads * dh), lambda b_, h, i: (b_, 0, h)),
                  _const_spec((tk, tk))],
        out_specs=pl.BlockSpec((None, tq, n_heads * dh), lambda b_, h, i: (b_, i, h)),
        out_shape=jax.ShapeDtypeStruct((b, t_q, d), BF16),
        scratch_shapes=[pltpu.VMEM((n_heads, tq, 128), F32), pltpu.VMEM((tq, n_heads * dh), F32)],
        compiler_params=_cparams(3),
        name="sb_attention",
    )(q, k, v, tri)


def _band_kernel(q_ref, k0_ref, k1_ref, k2_ref, v0_ref, v1_ref, v2_ref, tile_ref, o_ref, *, tk, pairs, prompt):
    tq = q_ref.shape[0]
    pair = 128
    half = pair // 2
    lane = lax.broadcasted_iota(jnp.int32, (1, pair), 1)
    col = lax.broadcasted_iota(jnp.int32, (1, 3 * tk), 1)

    def compute(first_col):
        scores = []
        for c in range(pairs):
            ln = pl.ds(c * pair, pair)
            q = q_ref[:, ln]
            kw = jnp.concatenate([k0_ref[:, ln], k1_ref[:, ln], k2_ref[:, ln]], axis=0)
            for hh in range(2):
                in_head = (lane >= hh * half) & (lane < (hh + 1) * half)
                s = _dot_nt(jnp.where(in_head, q, jnp.zeros_like(q)), kw) + tile_ref[2 * c + hh]
                if first_col is not None:
                    s = jnp.where(col >= first_col, s, -jnp.inf)
                scores.append(s)
        weights = []
        for s in scores:
            m = jnp.broadcast_to(jnp.max(_lane_tiles(s, jnp.maximum), axis=-1, keepdims=True), (tq, 128))
            weights.append(jnp.exp2(s - jnp.concatenate([m] * (3 * tk // 128), axis=1)).astype(BF16))
        for c in range(pairs):
            lv = pl.ds(c * 2 * pair, 2 * pair)
            vw = jnp.concatenate([v0_ref[:, lv], v1_ref[:, lv], v2_ref[:, lv]], axis=0)
            outs = []
            for hh in range(2):
                r = _dot(weights[2 * c + hh], vw)
                outs.append(r[:, :pair] / r[:, pair:])
            o_ref[:, pl.ds(c * pair, pair)] = jnp.where(lane < half, outs[0], outs[1]).astype(o_ref.dtype)

    if prompt:
        i = pl.program_id(2)

        @pl.when(i < 2)
        def _():
            compute((2 - i) * tk)

        @pl.when(i >= 2)
        def _():
            compute(None)
    else:
        compute(None)


def _band_tile(rel_bias, q0, n_q, k0, n_k, n_valid):
    heads = rel_bias.shape[0]
    n = n_q + n_k - 1
    dist = (q0 - k0) + (n_q - 1) - jnp.arange(n)
    g = rel_bias[:, jnp.clip(dist, -REL_MAX, REL_MAX) + REL_MAX].astype(F32)
    flat = jnp.tile(jnp.pad(g, ((0, 0), (0, 1))), (1, n_q))[:, :n_q * n]
    bias = flat.reshape(heads, n_q, n)[:, :, n_q - 1:]
    q_chunk = ((q0 + jnp.arange(n_q)) // CHUNK)[:, None]
    k_chunk = ((k0 + jnp.arange(n_k)) // CHUNK)[None, :]
    mask = (jnp.arange(n_k) < n_valid)[None, :] & (k_chunk <= q_chunk) & (k_chunk >= q_chunk - BAND_CHUNKS)
    return jnp.where(mask[None], bias, -jnp.inf)


def _band_attention(q, k, v, tile, heads, tq, tk, prompt):
    b, t_q, d = q.shape
    assert 2 * (d // heads) == 128
    pairs = 2
    w = 128 * pairs
    if prompt:
        kmap = lambda t: (lambda hp, b_, i: (b_, jnp.maximum(i - 2 + t, 0), hp))
    else:
        kmap = lambda t: (lambda hp, b_, i: (b_, t, hp))
    kspecs = [pl.BlockSpec((None, tk, w), kmap(t)) for t in range(3)]
    vspecs = [pl.BlockSpec((None, tk, 2 * w), kmap(t)) for t in range(3)]
    return pl.pallas_call(
        functools.partial(_band_kernel, tk=tk, pairs=pairs, prompt=prompt),
        grid=(heads // (2 * pairs), b, t_q // tq),
        in_specs=[pl.BlockSpec((None, tq, w), lambda hp, b_, i: (b_, i, hp))] + kspecs + vspecs
        + [pl.BlockSpec((2 * pairs, tq, 3 * tk), lambda hp, b_, i: (hp, 0, 0))],
        out_specs=pl.BlockSpec((None, tq, w), lambda hp, b_, i: (b_, i, hp)),
        out_shape=jax.ShapeDtypeStruct((b, t_q, d), BF16),
        compiler_params=_cparams(3),
        name="band_attention",
    )(q, k, k, k, v, v, v, tile)


def _mla_proj_kernel(x_ref, g_ref, wdq_ref, qn_ref, wqc_ref, wqs_ref, wdkv_ref, kvn_ref, wuk_ref, wuv_ref,
                     cos_ref, sin_ref, qc_ref, kc_ref, v_ref, ckv_ref, kr_ref, *, heads, scale):
    w = MLA_NOPE
    h = _rms(x_ref[...], g_ref[...]).astype(BF16)
    qn = _rms(_dot(h, wdq_ref[...]), qn_ref[...]).astype(BF16)
    a = _dot(qn, wqc_ref[...])
    sw = _dot(qn, wqs_ref[...])
    cos = cos_ref[...]
    sin = sin_ref[...]
    for hd in range(heads):
        qc_ref[:, 2 * w * hd:2 * w * hd + w] = (a[:, 2 * w * hd:2 * w * hd + w] * scale).astype(BF16)
        rope = a[:, 2 * w * hd + w:2 * w * (hd + 1)] * cos + sw[:, w * hd:w * (hd + 1)] * sin
        qc_ref[:, 2 * w * hd + w:2 * w * (hd + 1)] = (rope * scale).astype(BF16)
    kv = _dot(h, wdkv_ref[...])
    lora = kv.shape[1] - 2 * w
    ckv = _rms(kv[:, :lora], kvn_ref[...])
    ckv_ref[...] = ckv
    kr = kv[:, lora:lora + w] * cos + kv[:, lora + w:lora + 2 * w] * sin
    kr_ref[...] = kr[:, :MLA_ROPE]
    cb = ckv.astype(BF16)
    kn = _dot(cb, wuk_ref[...])
    vv = _dot(cb, wuv_ref[...])
    krb = kr.astype(BF16)
    ones = jnp.ones((krb.shape[0], w), BF16)
    for hd in range(heads):
        kc_ref[:, 2 * w * hd:2 * w * hd + w] = kn[:, w * hd:w * (hd + 1)].astype(BF16)
        kc_ref[:, 2 * w * hd + w:2 * w * (hd + 1)] = krb
        v_ref[:, 2 * w * hd:2 * w * hd + w] = vv[:, w * hd:w * (hd + 1)].astype(BF16)
        v_ref[:, 2 * w * hd + w:2 * w * (hd + 1)] = ones


def _mla_proj(x2d, g, wts, cos_t, sin_t, heads, scale, tm):
    n, d = x2d.shape
    wdq, qn, wqc, wqs, wdkv, kvn, wuk, wuv = wts
    lora = wuk.shape[0]
    n_pos = cos_t.shape[0] // tm
    row = lambda i: (i, 0)
    pos = lambda i: (i % n_pos, 0)
    return pl.pallas_call(
        functools.partial(_mla_proj_kernel, heads=heads, scale=scale),
        grid=(n // tm,),
        in_specs=[pl.BlockSpec((tm, d), row), _const_spec((1, d)), _const_spec(wdq.shape), _const_spec(qn.shape),
                  _const_spec(wqc.shape), _const_spec(wqs.shape), _const_spec(wdkv.shape), _const_spec(kvn.shape),
                  _const_spec(wuk.shape), _const_spec(wuv.shape),
                  pl.BlockSpec((tm, MLA_NOPE), pos), pl.BlockSpec((tm, MLA_NOPE), pos)],
        out_specs=[pl.BlockSpec((tm, heads * 2 * MLA_NOPE), row), pl.BlockSpec((tm, heads * 2 * MLA_NOPE), row),
                   pl.BlockSpec((tm, heads * 2 * MLA_V), row), pl.BlockSpec((tm, lora), row),
                   pl.BlockSpec((tm, MLA_ROPE), row)],
        out_shape=[jax.ShapeDtypeStruct((n, heads * 2 * MLA_NOPE), BF16),
                   jax.ShapeDtypeStruct((n, heads * 2 * MLA_NOPE), BF16),
                   jax.ShapeDtypeStruct((n, heads * 2 * MLA_V), BF16),
                   jax.ShapeDtypeStruct((n, lora), F32),
                   jax.ShapeDtypeStruct((n, MLA_ROPE), F32)],
        compiler_params=_cparams(1),
        name="mla_proj",
    )(x2d, g.reshape(1, d), wdq, qn, wqc, wqs, wdkv, kvn, wuk, wuv, cos_t, sin_t)


def _mla_expand_kernel(c_ref, r_ref, wuk_ref, wuv_ref, kc_ref, v_ref, *, heads):
    w = MLA_NOPE
    cb = c_ref[...].astype(BF16)
    kn = _dot(cb, wuk_ref[...])
    vv = _dot(cb, wuv_ref[...])
    krb = r_ref[...].astype(BF16)
    zeros = jnp.zeros((krb.shape[0], w - MLA_ROPE), BF16)
    ones = jnp.ones((krb.shape[0], w), BF16)
    for hd in range(heads):
        kc_ref[:, 2 * w * hd:2 * w * hd + w] = kn[:, w * hd:w * (hd + 1)].astype(BF16)
        kc_ref[:, 2 * w * hd + w:2 * w * hd + w + MLA_ROPE] = krb
        kc_ref[:, 2 * w * hd + w + MLA_ROPE:2 * w * (hd + 1)] = zeros
        v_ref[:, 2 * w * hd:2 * w * hd + w] = vv[:, w * hd:w * (hd + 1)].astype(BF16)
        v_ref[:, 2 * w * hd + w:2 * w * (hd + 1)] = ones


def _mla_expand(c2d, r2d, wuk, wuv, heads, tm):
    n, lora = c2d.shape
    row = lambda i: (i, 0)
    return pl.pallas_call(
        functools.partial(_mla_expand_kernel, heads=heads),
        grid=(n // tm,),
        in_specs=[pl.BlockSpec((tm, lora), row), pl.BlockSpec((tm, MLA_ROPE), row),
                  _const_spec(wuk.shape), _const_spec(wuv.shape)],
        out_specs=[pl.BlockSpec((tm, heads * 2 * MLA_NOPE), row), pl.BlockSpec((tm, heads * 2 * MLA_V), row)],
        out_shape=[jax.ShapeDtypeStruct((n, heads * 2 * MLA_NOPE), BF16),
                   jax.ShapeDtypeStruct((n, heads * 2 * MLA_V), BF16)],
        compiler_params=_cparams(1),
        name="mla_expand",
    )(c2d, r2d, wuk, wuv)


def _mla_attn_kernel(q_ref, kt_ref, v_ref, o_ref, m_ref, alpha_ref, p_ref, acc_ref, *, tq, tk, group, q_off, kv_len,
                     n_mask, mask_all):
    i = pl.program_id(2)
    q_start = q_off + i * tq
    q_chunk = (q_start + lax.broadcasted_iota(jnp.int32, (tq, 1), 0)) // CHUNK
    col = lax.broadcasted_iota(jnp.int32, (1, tk), 1)
    k_end = jnp.minimum(((q_start + tq - 1) // CHUNK + 1) * CHUNK, kv_len)
    n_blk = (k_end + tk - 1) // tk
    m_ref[...] = jnp.full(m_ref.shape, -jnp.inf, F32)
    acc_ref[...] = jnp.zeros(acc_ref.shape, F32)
    strip = min(tq, 64)

    n_rows = 2 if tq >= 256 else 1
    rh = tq // n_rows

    def process(blocks):
        starts = [pl.multiple_of(j * tk, tk) for j, _ in blocks]
        scores = [[_dot(q_ref[pl.ds(h * rh, rh), :], kt_ref[:, pl.ds(st, tk)]) for h in range(n_rows)]
                  for st in starts]
        for n, ((_, masked), st, s_halves) in enumerate(zip(blocks, starts, scores)):
            c = n % 2
            for r0 in range(0, tq, strip):
                rows = pl.ds(r0, strip)
                s = s_halves[r0 // rh][r0 % rh:r0 % rh + strip, :]
                if masked:
                    k_pos = st + col
                    s = jnp.where(((k_pos // CHUNK) <= q_chunk[r0:r0 + strip]) & (k_pos < kv_len), s, -jnp.inf)
                m_prev = m_ref[c, rows, :]
                m_new = jnp.maximum(m_prev, jnp.max(_lane_tiles(s, jnp.maximum), axis=-1, keepdims=True))
                p_ref[n, rows, :] = jnp.exp2(s - jnp.concatenate([m_new] * (tk // 128), axis=1)).astype(BF16)
                alpha_ref[n, rows, :] = jnp.exp2(m_prev - m_new)
                m_ref[c, rows, :] = m_new
        for n, st in enumerate(starts):
            c = n % 2
            for h in range(n_rows):
                rows = pl.ds(h * rh, rh)
                alpha = alpha_ref[n, rows, :]
                acc_ref[c, rows, :] = (jnp.concatenate([alpha, alpha], axis=1) * acc_ref[c, rows, :]
                                       + _dot(p_ref[n, rows, :], v_ref[pl.ds(st, tk), :]))

    def body(g, carry):
        process([(group * g + n, mask_all) for n in range(group)])
        return carry

    n_old = n_blk - n_mask
    n_groups = n_old // group
    lax.fori_loop(0, n_groups, body, 0)
    first = group * n_groups
    for rest in range(group):
        @pl.when(n_old - first == rest)
        def _(rest=rest):
            process([(first + n, mask_all) for n in range(rest)]
                    + [(first + rest + n, True) for n in range(n_mask)])

    m0, m1 = m_ref[0], m_ref[1]
    m = jnp.maximum(m0, m1)
    a0 = jnp.exp2(m0 - m)
    a1 = jnp.exp2(m1 - m)
    acc = jnp.concatenate([a0, a0], axis=1) * acc_ref[0] + jnp.concatenate([a1, a1], axis=1) * acc_ref[1]
    o_ref[...] = (acc[:, :MLA_V] / acc[:, MLA_V:]).astype(o_ref.dtype)


def _mla_attention(qc, kc, v1, heads, tq, tk, q_off, kv_len):
    b, t_q, _ = qc.shape
    t_k = kc.shape[1]
    w = 2 * MLA_NOPE
    mask_all = not (q_off == 0 and tq % tk == 0 and t_q == t_k == kv_len)
    n_mask = 1 if mask_all else tq // tk
    group = 4
    slots = group - 1 + n_mask
    kt = jnp.transpose(kc, (0, 2, 1))
    return pl.pallas_call(
        functools.partial(_mla_attn_kernel, tq=tq, tk=tk, group=group, q_off=q_off, kv_len=kv_len,
                          n_mask=n_mask, mask_all=mask_all),
        grid=(b, heads, t_q // tq),
        in_specs=[pl.BlockSpec((None, tq, w), lambda b_, h, i: (b_, i, h)),
                  pl.BlockSpec((None, w, t_k), lambda b_, h, i: (b_, h, 0)),
                  pl.BlockSpec((None, t_k, MLA_V + 128), lambda b_, h, i: (b_, 0, h))],
        out_specs=pl.BlockSpec((None, tq, MLA_V), lambda b_, h, i: (b_, i, h)),
        out_shape=jax.ShapeDtypeStruct((b, t_q, heads * MLA_V), BF16),
        scratch_shapes=[pltpu.VMEM((2, tq, 128), F32), pltpu.VMEM((slots, tq, 128), F32),
                        pltpu.VMEM((slots, tq, tk), BF16),
                        pltpu.VMEM((2, tq, MLA_V + 128), F32)],
        compiler_params=_cparams(3),
        name="mla_attention",
    )(qc, kt, v1)


def _rope_tables(pos):
    half = MLA_ROPE // 2
    inv_freq = ROPE_THETA ** (-jnp.arange(half, dtype=F32) / half)
    ang = pos.astype(F32)[:, None] * inv_freq[None, :]
    zeros = jnp.zeros((pos.shape[0], MLA_NOPE - MLA_ROPE), F32)
    cos, sin = jnp.cos(ang), jnp.sin(ang)
    return jnp.concatenate([cos, cos, zeros], axis=1), jnp.concatenate([sin, sin, zeros], axis=1)


def _swap_halves(w):
    half = w.shape[-1] // 2
    return jnp.concatenate([-w[..., half:], w[..., :half]], axis=-1)


def _mla_weights(w_dq, q_norm, w_uq, w_dkv, kv_norm, w_uk, w_uv, heads):
    q_lora = w_dq.shape[1]
    lora = w_uk.shape[0]
    pad = MLA_NOPE - MLA_ROPE
    uq = w_uq.reshape(q_lora, heads, MLA_NOPE + MLA_ROPE)
    zq = jnp.zeros((q_lora, heads, pad), F32)
    wqc = jnp.concatenate([uq, zq], axis=-1).reshape(q_lora, heads * 2 * MLA_NOPE)
    wqs = jnp.concatenate([_swap_halves(uq[..., MLA_NOPE:]), zq], axis=-1).reshape(q_lora, heads * MLA_NOPE)
    zk = jnp.zeros((w_dkv.shape[0], pad), F32)
    rope = w_dkv[:, lora:]
    wdkv = jnp.concatenate([w_dkv[:, :lora], rope, zk, _swap_halves(rope), zk], axis=1)
    return (w_dq.astype(BF16), q_norm.reshape(1, q_lora), wqc.astype(BF16), wqs.astype(BF16), wdkv.astype(BF16),
            kv_norm.reshape(1, lora), w_uk.reshape(lora, heads * MLA_NOPE).astype(BF16),
            w_uv.reshape(lora, heads * MLA_V).astype(BF16))


def _pad_rows(x, rows):
    return jnp.pad(x, ((0, 0), (0, rows - x.shape[1]), (0, 0)))


def kernel(x_prompt, x_sample, cache_sb_k, cache_sb_v, cache_band_k, cache_band_v, cache_mla_ckv, cache_mla_krope, norm_mix, norm_ffn, norm_final, w_gate_up, w_down, sb_w_qkv, sb_w_o, band_w_qkv, band_rel_bias, band_w_o, mla_w_dq, mla_q_norm, mla_w_uq, mla_w_dkv, mla_kv_norm, mla_w_uk, mla_w_uv, mla_w_o):
    bp, seq, d = x_prompt.shape
    bs, dec, _ = x_sample.shape
    depth = norm_mix.shape[0]
    past = cache_sb_k.shape[2]
    sb_heads, sb_dh = cache_sb_k.shape[3], cache_sb_k.shape[4]
    band_heads, band_dh = cache_band_k.shape[3], cache_band_k.shape[4]
    band_rows = cache_band_k.shape[2]
    mla_heads = mla_w_uk.shape[2]
    lora = mla_w_uk.shape[1]
    mla_scale = (MLA_NOPE + MLA_ROPE) ** -0.5 * LOG2E

    tm_p, tm_s = 512, bs * dec
    tm_ffn = 512
    blk = 256
    mla_blk = 512
    fc = 256

    xp = x_prompt.reshape(bp * seq, d)
    xs = x_sample.reshape(bs * dec, d)
    outs = {name: [] for name in ("sb_kp", "sb_vp", "sb_ks", "sb_vs", "band_kp", "band_vp", "band_ks", "band_vs",
                                  "mla_cp", "mla_rp", "mla_cs", "mla_rs")}

    for layer in range(depth):
        kind, j = layer % 3, layer // 3
        g_mix = norm_mix[layer]
        if kind == 0:
            w_bf = sb_w_qkv[j].astype(BF16)
            scale = sb_dh ** -0.5 * LOG2E
            q, k, v, kb, vb = _qkv_proj(xp, g_mix, w_bf, scale, tm_p)
            ap = _sb_attention(q.reshape(bp, seq, d), kb.reshape(bp, seq, d), vb.reshape(bp, seq, d),
                               sb_heads, blk, blk, 0)
            outs["sb_kp"].append(k.reshape(bp, seq, sb_heads, sb_dh))
            outs["sb_vp"].append(v.reshape(bp, seq, sb_heads, sb_dh))
            q, k, v, kb, vb = _qkv_proj(xs, g_mix, w_bf, scale, tm_s)
            t_k = -(-(past + dec) // blk) * blk
            k_all, v_all = _sb_keys(cache_sb_k, cache_sb_v, j, kb.reshape(bs, dec, d), vb.reshape(bs, dec, d), t_k)
            a_s = _sb_attention(q.reshape(bs, dec, d), k_all, v_all, sb_heads, dec, blk, past)
            outs["sb_ks"].append(k.reshape(bs, dec, sb_heads, sb_dh))
            outs["sb_vs"].append(v.reshape(bs, dec, sb_heads, sb_dh))
            w_o = sb_w_o[j]
        elif kind == 1:
            w_bf = band_w_qkv[j].astype(BF16)
            scale = band_dh ** -0.5 * LOG2E
            win = BAND_CHUNKS * CHUNK
            q, k, v, kb, vb = _qkv_proj(xp, g_mix, w_bf, scale, tm_p, ones_v=True)
            tile = _band_tile(band_rel_bias[j], win, blk, 0, 3 * blk, 3 * blk) * LOG2E
            ap = _band_attention(q.reshape(bp, seq, d), kb.reshape(bp, seq, d), vb.reshape(bp, seq, 2 * d), tile,
                                 band_heads, blk, blk, True)
            keep = min(win, seq)
            outs["band_kp"].append(k.reshape(bp, seq, d)[:, seq - keep:].reshape(bp, keep, band_heads, band_dh))
            outs["band_vp"].append(v.reshape(bp, seq, d)[:, seq - keep:].reshape(bp, keep, band_heads, band_dh))
            q, k, v, kb, vb = _qkv_proj(xs, g_mix, w_bf, scale, tm_s, ones_v=True)
            n_keys = band_rows + dec
            k_all = _pad_rows(jnp.concatenate([cache_band_k[j].reshape(bs, band_rows, d).astype(BF16),
                                               kb.reshape(bs, dec, d)], axis=1), 3 * blk)
            cache_v = cache_band_v[j].reshape(bs, band_rows, d // 128, 128).astype(BF16)
            cache_v = jnp.concatenate([cache_v, jnp.ones_like(cache_v)], axis=-1).reshape(bs, band_rows, 2 * d)
            v_all = _pad_rows(jnp.concatenate([cache_v, vb.reshape(bs, dec, 2 * d)], axis=1), 3 * blk)
            tile = _band_tile(band_rel_bias[j], past, dec, past - band_rows, 3 * blk, n_keys) * LOG2E
            a_s = _band_attention(q.reshape(bs, dec, d), k_all, v_all, tile, band_heads, dec, blk, False)
            outs["band_ks"].append(k.reshape(bs, dec, band_heads, band_dh))
            outs["band_vs"].append(v.reshape(bs, dec, band_heads, band_dh))
            w_o = band_w_o[j]
        else:
            wts = _mla_weights(mla_w_dq[j], mla_q_norm[j], mla_w_uq[j], mla_w_dkv[j], mla_kv_norm[j],
                               mla_w_uk[j], mla_w_uv[j], mla_heads)
            cos_p, sin_p = _rope_tables(jnp.arange(seq))
            qc, kc, vv, ckv, kr = _mla_proj(xp, g_mix, wts, cos_p, sin_p, mla_heads, mla_scale, tm_p)
            ap = _mla_attention(qc.reshape(bp, seq, -1), kc.reshape(bp, seq, -1), vv.reshape(bp, seq, -1),
                                mla_heads, mla_blk, mla_blk, 0, seq)
            outs["mla_cp"].append(ckv.reshape(bp, seq, lora))
            outs["mla_rp"].append(kr.reshape(bp, seq, MLA_ROPE))
            cos_s, sin_s = _rope_tables(jnp.tile(past + jnp.arange(dec), bs))
            qc, kc, vv, ckv, kr = _mla_proj(xs, g_mix, wts, cos_s, sin_s, mla_heads, mla_scale, tm_s)
            kc_c, v_c = _mla_expand(cache_mla_ckv[j].reshape(bs * past, lora),
                                    cache_mla_krope[j].reshape(bs * past, MLA_ROPE), wts[6], wts[7], mla_heads, blk)
            t_k = -(-(past + dec) // blk) * blk
            kc_all = _pad_rows(jnp.concatenate([kc_c.reshape(bs, past, -1), kc.reshape(bs, dec, -1)], axis=1), t_k)
            v_all = _pad_rows(jnp.concatenate([v_c.reshape(bs, past, -1), vv.reshape(bs, dec, -1)], axis=1), t_k)
            a_s = _mla_attention(qc.reshape(bs, dec, -1), kc_all, v_all, mla_heads, dec, blk, past, past + dec)
            outs["mla_cs"].append(ckv.reshape(bs, dec, lora))
            outs["mla_rs"].append(kr.reshape(bs, dec, MLA_ROPE))
            w_o = mla_w_o[j]

        wgu, wd = w_gate_up[layer].astype(BF16), w_down[layer].astype(BF16)
        final = layer == depth - 1
        wo_bf = w_o.astype(BF16)
        xp = _oproj_ffn(xp, ap.reshape(bp * seq, d), wo_bf, norm_ffn[layer], wgu, wd, norm_final, final, tm_ffn, fc)
        xs = _oproj_ffn(xs, a_s.reshape(bs * dec, d), wo_bf, norm_ffn[layer], wgu, wd, norm_final, final, tm_s, fc)

    st = lambda name: jnp.stack(outs[name])
    return (xp.reshape(bp, seq, d), xs.reshape(bs, dec, d),
            st("sb_kp"), st("sb_vp"), st("band_kp"), st("band_vp"), st("mla_cp"), st("mla_rp"),
            st("sb_ks"), st("sb_vs"), st("band_ks"), st("band_vs"), st("mla_cs"), st("mla_rs"))
```

```python
import functools

import jax
import jax.numpy as jnp
from jax import lax
from jax.experimental import pallas as pl
from jax.experimental.pallas import tpu as pltpu

CHUNK = 64
BAND_CHUNKS = 8
REL_MAX = 128
MLA_NOPE = 128
MLA_ROPE = 64
MLA_V = 128
ROPE_THETA = 10000.0
EPS = 1e-6

LOG2E = 1.4426950408889634
SB_LOG2_ZERO = -151.0

VMEM_LIMIT = 56 * 1024 * 1024
F32 = jnp.float32
BF16 = jnp.bfloat16


def _cparams(n_grid):
    return pltpu.CompilerParams(dimension_semantics=("arbitrary",) * n_grid, vmem_limit_bytes=VMEM_LIMIT)


def _rms(x, g):
    return x * lax.rsqrt(jnp.mean(x * x, axis=-1, keepdims=True) + EPS) * g


def _dot(a, b):
    return jnp.dot(a, b, preferred_element_type=F32)


def _dot_nt(a, b):
    return lax.dot_general(a, b, (((1,), (1,)), ((), ())), preferred_element_type=F32)


def _lane_tiles(x, op):
    tiles = [x[:, t:t + 128] for t in range(0, x.shape[1], 128)]
    while len(tiles) > 1:
        tiles = [op(tiles[t], tiles[t + 1]) for t in range(0, len(tiles) - 1, 2)] + tiles[len(tiles) & ~1:]
    return tiles[0]


def _const_spec(shape):
    nd = len(shape)
    return pl.BlockSpec(shape, lambda *_: (0,) * nd)


def _qkv_kernel(x_ref, g_ref, w_ref, q_ref, k_ref, v_ref, kb_ref, vb_ref, *, d, q_scale, ones_v):
    h = _rms(x_ref[...], g_ref[...]).astype(BF16)
    q = _dot(h, w_ref[:, 0:d])
    q_ref[...] = (q * q_scale).astype(BF16)
    k = _dot(h, w_ref[:, d:2 * d])
    k_ref[...] = k
    kb_ref[...] = k.astype(BF16)
    v = _dot(h, w_ref[:, 2 * d:3 * d])
    v_ref[...] = v
    if ones_v:
        ones = jnp.ones((k.shape[0], 128), BF16)
        for t in range(d // 128):
            vb_ref[:, 256 * t:256 * t + 128] = v[:, 128 * t:128 * (t + 1)].astype(BF16)
            vb_ref[:, 256 * t + 128:256 * (t + 1)] = ones
    else:
        vb_ref[...] = v.astype(BF16)


def _qkv_proj(x2d, g, w_bf, q_scale, tm, ones_v=False):
    n, d = x2d.shape
    row = lambda i: (i, 0)
    wv = 2 * d if ones_v else d
    return pl.pallas_call(
        functools.partial(_qkv_kernel, d=d, q_scale=q_scale, ones_v=ones_v),
        grid=(n // tm,),
        in_specs=[pl.BlockSpec((tm, d), row), _const_spec((1, d)), _const_spec((d, 3 * d))],
        out_specs=[pl.BlockSpec((tm, d), row)] * 4 + [pl.BlockSpec((tm, wv), row)],
        out_shape=[jax.ShapeDtypeStruct((n, d), BF16), jax.ShapeDtypeStruct((n, d), F32),
                   jax.ShapeDtypeStruct((n, d), F32), jax.ShapeDtypeStruct((n, d), BF16),
                   jax.ShapeDtypeStruct((n, wv), BF16)],
        compiler_params=_cparams(1),
        name="qkv_proj",
    )(x2d, g.reshape(1, d), w_bf)


def _ffn_kernel(x_ref, a_ref, wo_ref, g_ref, wgu_ref, wd_ref, gf_ref, o_ref, acc_ref, h_ref, *, n_chunks, fc, final):
    x1 = x_ref[...] + _dot(a_ref[...], wo_ref[...])
    h_ref[...] = _rms(x1, g_ref[...]).astype(BF16)
    acc_ref[...] = x1

    def body(c, carry):
        lo = pl.multiple_of(c * fc, fc)
        gate = _dot(h_ref[...], wgu_ref[:, pl.ds(lo, fc)])
        up = _dot(h_ref[...], wgu_ref[:, pl.ds(pl.multiple_of(n_chunks * fc + lo, fc), fc)])
        act = (gate / (1.0 + jnp.exp(-gate)) * up).astype(BF16)
        acc_ref[...] += _dot(act, wd_ref[pl.ds(lo, fc), :])
        return carry

    lax.fori_loop(0, n_chunks, body, 0, unroll=True)
    out = acc_ref[...]
    if final:
        out = _rms(out, gf_ref[...])
    o_ref[...] = out


def _oproj_ffn(x2d, a2d, wo_bf, g, wgu_c, wd_c, g_final, final, tm, fc):
    n, d = x2d.shape
    n_chunks = wd_c.shape[0] // fc
    row = lambda i: (i, 0)
    return pl.pallas_call(
        functools.partial(_ffn_kernel, n_chunks=n_chunks, fc=fc, final=final),
        grid=(n // tm,),
        in_specs=[pl.BlockSpec((tm, d), row), pl.BlockSpec((tm, d), row), _const_spec((d, d)), _const_spec((1, d)),
                  _const_spec(wgu_c.shape), _const_spec(wd_c.shape), _const_spec((1, d))],
        out_specs=pl.BlockSpec((tm, d), row),
        out_shape=jax.ShapeDtypeStruct((n, d), F32),
        scratch_shapes=[pltpu.VMEM((tm, d), F32), pltpu.VMEM((tm, d), BF16)],
        compiler_params=_cparams(1),
        name="oproj_ffn",
    )(x2d, a2d, wo_bf, g.reshape(1, d), wgu_c, wd_c, g_final.reshape(1, d))


def _sb_kernel(q_ref, k_ref, v_ref, tri_ref, o_ref, a_ref, acc_ref, *, tq, tk, dh, n_heads, q_off):
    i = pl.program_id(2)
    q_start = q_off + i * tq
    q_pos = q_start + lax.broadcasted_iota(jnp.int32, (tq, 1), 0)
    col = lax.broadcasted_iota(jnp.int32, (1, tk), 1)
    j_last = (q_start + tq - 2) // tk
    a_ref[...] = jnp.zeros(a_ref.shape, F32)
    acc_ref[...] = jnp.zeros(acc_ref.shape, F32)

    def cond(state):
        j, a_max = state
        return jnp.logical_and(j >= 0, a_max > SB_LOG2_ZERO)

    def body(state):
        j, _ = state
        tri = tri_ref[...]
        lanes = [pl.ds(c * dh, dh) for c in range(n_heads)]
        starts = [pl.multiple_of(j * tk, tk), pl.multiple_of(jnp.maximum(j - 1, 0) * tk, tk)]
        masks = [(starts[0] + col) < q_pos, jnp.logical_and((starts[1] + col) < q_pos, j >= 1)]
        zs = [[jnp.where(mask, _dot_nt(q_ref[:, ln], k_ref[pl.ds(st, tk), ln]), -jnp.inf) for ln in lanes]
              for st, mask in zip(starts, masks)]
        softs = [[jnp.maximum(z, 0.0) + jnp.log(1.0 + jnp.exp2(-jnp.abs(z))) * LOG2E for z in zb] for zb in zs]
        rs = []
        for sb in softs:
            rb = []
            for soft in sb:
                rb.append(_dot(soft.astype(BF16), tri))
            rs.append(rb)
        a_max = jnp.float32(-jnp.inf)
        ws = [[], []]
        for c in range(n_heads):
            a = a_ref[c]
            for blk in range(2):
                a_wide = jnp.concatenate([a] * (tk // 128), axis=1)
                ws[blk].append(jnp.exp2(zs[blk][c] + rs[blk][c] + a_wide).astype(BF16))
                a = a - jnp.sum(_lane_tiles(softs[blk][c], jnp.add), axis=-1, keepdims=True)
            a_ref[c] = a
            a_max = jnp.maximum(a_max, jnp.max(a))
        for c in range(n_heads):
            acc_ref[:, lanes[c]] += (_dot(ws[0][c], v_ref[pl.ds(starts[0], tk), lanes[c]])
                                     + _dot(ws[1][c], v_ref[pl.ds(starts[1], tk), lanes[c]]))
        return j - 2, a_max

    lax.while_loop(cond, body, (j_last, jnp.float32(0.0)))
    o_ref[...] = acc_ref[...].astype(o_ref.dtype)


def _sb_keys_kernel(ck_ref, cv_ref, nk_ref, nv_ref, k_ref, v_ref, *, heads, past, dec):
    dh = ck_ref.shape[2]
    for c_ref, n_ref, o_ref in ((ck_ref, nk_ref, k_ref), (cv_ref, nv_ref, v_ref)):
        for hd in range(heads):
            o_ref[0:past, hd * dh:(hd + 1) * dh] = c_ref[:, hd, :].astype(BF16)
        o_ref[past:past + dec, :] = n_ref[...]
        o_ref[past + dec:, :] = jnp.zeros((o_ref.shape[0] - past - dec, o_ref.shape[1]), BF16)


def _sb_keys(cache_k, cache_v, layer, new_k, new_v, t_k):
    _, b, past, heads, dh = cache_k.shape
    dec, d = new_k.shape[1], new_k.shape[2]
    cache_spec = pl.BlockSpec((None, None, past, heads, dh), lambda i: (layer, i, 0, 0, 0))
    new_spec = pl.BlockSpec((None, dec, d), lambda i: (i, 0, 0))
    out_spec = pl.BlockSpec((None, t_k, d), lambda i: (i, 0, 0))
    return pl.pallas_call(
        functools.partial(_sb_keys_kernel, heads=heads, past=past, dec=dec),
        grid=(b,),
        in_specs=[cache_spec, cache_spec, new_spec, new_spec],
        out_specs=[out_spec, out_spec],
        out_shape=[jax.ShapeDtypeStruct((b, t_k, d), BF16)] * 2,
        compiler_params=_cparams(1),
        name="sb_keys",
    )(cache_k, cache_v, new_k, new_v)


def _sb_attention(q, k, v, heads, tq, tk, q_off):
    b, t_q, d = q.shape
    t_k = k.shape[1]
    dh = d // heads
    n_heads = 2
    tri = -jnp.tril(jnp.ones((tk, tk), F32)).astype(BF16)
    return pl.pallas_call(
        functools.partial(_sb_kernel, tq=tq, tk=tk, dh=dh, n_heads=n_heads, q_off=q_off),
        grid=(b, heads // n_heads, t_q // tq),
        in_specs=[pl.BlockSpec((None, tq, n_heads * dh), lambda b_, h, i: (b_, i, h)),
                  pl.BlockSpec((None, t_k, n_heads * dh), lambda b_, h, i: (b_, 0, h)),
                  pl.BlockSpec((None, t_k, n_heads * dh), lambda b_, h, i: (b_, 0, h)),
                  _const_spec((tk, tk))],
        out_specs=pl.BlockSpec((None, tq, n_heads * dh), lambda b_, h, i: (b_, i, h)),
        out_shape=jax.ShapeDtypeStruct((b, t_q, d), BF16),
        scratch_shapes=[pltpu.VMEM((n_heads, tq, 128), F32), pltpu.VMEM((tq, n_heads * dh), F32)],
        compiler_params=_cparams(3),
        name="sb_attention",
    )(q, k, v, tri)


def _band_kernel(q_ref, k0_ref, k1_ref, k2_ref, v0_ref, v1_ref, v2_ref, tile_ref, o_ref, *, tk, pairs, prompt):
    tq = q_ref.shape[0]
    pair = 128
    half = pair // 2
    lane = lax.broadcasted_iota(jnp.int32, (1, pair), 1)
    col = lax.broadcasted_iota(jnp.int32, (1, 3 * tk), 1)

    def compute(first_col):
        scores = []
        for c in range(pairs):
            ln = pl.ds(c * pair, pair)
            q = q_ref[:, ln]
            kw = jnp.concatenate([k0_ref[:, ln], k1_ref[:, ln], k2_ref[:, ln]], axis=0)
            for hh in range(2):
                in_head = (lane >= hh * half) & (lane < (hh + 1) * half)
                s = _dot_nt(jnp.where(in_head, q, jnp.zeros_like(q)), kw) + tile_ref[2 * c + hh]
                if first_col is not None:
                    s = jnp.where(col >= first_col, s, -jnp.inf)
                scores.append(s)
        weights = []
        for s in scores:
            m = jnp.broadcast_to(jnp.max(_lane_tiles(s, jnp.maximum), axis=-1, keepdims=True), (tq, 128))
            weights.append(jnp.exp2(s - jnp.concatenate([m] * (3 * tk // 128), axis=1)).astype(BF16))
        for c in range(pairs):
            lv = pl.ds(c * 2 * pair, 2 * pair)
            vw = jnp.concatenate([v0_ref[:, lv], v1_ref[:, lv], v2_ref[:, lv]], axis=0)
            outs = []
            for hh in range(2):
                r = _dot(weights[2 * c + hh], vw)
                outs.append(r[:, :pair] / r[:, pair:])
            o_ref[:, pl.ds(c * pair, pair)] = jnp.where(lane < half, outs[0], outs[1]).astype(o_ref.dtype)

    if prompt:
        i = pl.program_id(2)

        @pl.when(i < 2)
        def _():
            compute((2 - i) * tk)

        @pl.when(i >= 2)
        def _():
            compute(None)
    else:
        compute(None)


def _band_tile(rel_bias, q0, n_q, k0, n_k, n_valid):
    heads = rel_bias.shape[0]
    n = n_q + n_k - 1
    dist = (q0 - k0) + (n_q - 1) - jnp.arange(n)
    g = rel_bias[:, jnp.clip(dist, -REL_MAX, REL_MAX) + REL_MAX].astype(F32)
    flat = jnp.tile(jnp.pad(g, ((0, 0), (0, 1))), (1, n_q))[:, :n_q * n]
    bias = flat.reshape(heads, n_q, n)[:, :, n_q - 1:]
    q_chunk = ((q0 + jnp.arange(n_q)) // CHUNK)[:, None]
    k_chunk = ((k0 + jnp.arange(n_k)) // CHUNK)[None, :]
    mask = (jnp.arange(n_k) < n_valid)[None, :] & (k_chunk <= q_chunk) & (k_chunk >= q_chunk - BAND_CHUNKS)
    return jnp.where(mask[None], bias, -jnp.inf)


def _band_attention(q, k, v, tile, heads, tq, tk, prompt):
    b, t_q, d = q.shape
    assert 2 * (d // heads) == 128
    pairs = 2
    w = 128 * pairs
    if prompt:
        kmap = lambda t: (lambda hp, b_, i: (b_, jnp.maximum(i - 2 + t, 0), hp))
    else:
        kmap = lambda t: (lambda hp, b_, i: (b_, t, hp))
    kspecs = [pl.BlockSpec((None, tk, w), kmap(t)) for t in range(3)]
    vspecs = [pl.BlockSpec((None, tk, 2 * w), kmap(t)) for t in range(3)]
    return pl.pallas_call(
        functools.partial(_band_kernel, tk=tk, pairs=pairs, prompt=prompt),
        grid=(heads // (2 * pairs), b, t_q // tq),
        in_specs=[pl.BlockSpec((None, tq, w), lambda hp, b_, i: (b_, i, hp))] + kspecs + vspecs
        + [pl.BlockSpec((2 * pairs, tq, 3 * tk), lambda hp, b_, i: (hp, 0, 0))],
        out_specs=pl.BlockSpec((None, tq, w), lambda hp, b_, i: (b_, i, hp)),
        out_shape=jax.ShapeDtypeStruct((b, t_q, d), BF16),
        compiler_params=_cparams(3),
        name="band_attention",
    )(q, k, k, k, v, v, v, tile)


def _mla_proj_kernel(x_ref, g_ref, wdq_ref, qn_ref, wqc_ref, wqs_ref, wdkv_ref, kvn_ref, wuk_ref, wuv_ref,
                     cos_ref, sin_ref, qc_ref, kc_ref, v_ref, ckv_ref, kr_ref, *, heads, scale):
    w = MLA_NOPE
    h = _rms(x_ref[...], g_ref[...]).astype(BF16)
    qn = _rms(_dot(h, wdq_ref[...]), qn_ref[...]).astype(BF16)
    a = _dot(qn, wqc_ref[...])
    sw = _dot(qn, wqs_ref[...])
    cos = cos_ref[...]
    sin = sin_ref[...]
    for hd in range(heads):
        qc_ref[:, 2 * w * hd:2 * w * hd + w] = (a[:, 2 * w * hd:2 * w * hd + w] * scale).astype(BF16)
        rope = a[:, 2 * w * hd + w:2 * w * (hd + 1)] * cos + sw[:, w * hd:w * (hd + 1)] * sin
        qc_ref[:, 2 * w * hd + w:2 * w * (hd + 1)] = (rope * scale).astype(BF16)
    kv = _dot(h, wdkv_ref[...])
    lora = kv.shape[1] - 2 * w
    ckv = _rms(kv[:, :lora], kvn_ref[...])
    ckv_ref[...] = ckv
    kr = kv[:, lora:lora + w] * cos + kv[:, lora + w:lora + 2 * w] * sin
    kr_ref[...] = kr[:, :MLA_ROPE]
    cb = ckv.astype(BF16)
    kn = _dot(cb, wuk_ref[...])
    vv = _dot(cb, wuv_ref[...])
    krb = kr.astype(BF16)
    ones = jnp.ones((krb.shape[0], w), BF16)
    for hd in range(heads):
        kc_ref[:, 2 * w * hd:2 * w * hd + w] = kn[:, w * hd:w * (hd + 1)].astype(BF16)
        kc_ref[:, 2 * w * hd + w:2 * w * (hd + 1)] = krb
        v_ref[:, 2 * w * hd:2 * w * hd + w] = vv[:, w * hd:w * (hd + 1)].astype(BF16)
        v_ref[:, 2 * w * hd + w:2 * w * (hd + 1)] = ones


def _mla_proj(x2d, g, wts, cos_t, sin_t, heads, scale, tm):
    n, d = x2d.shape
    wdq, qn, wqc, wqs, wdkv, kvn, wuk, wuv = wts
    lora = wuk.shape[0]
    n_pos = cos_t.shape[0] // tm
    row = lambda i: (i, 0)
    pos = lambda i: (i % n_pos, 0)
    return pl.pallas_call(
        functools.partial(_mla_proj_kernel, heads=heads, scale=scale),
        grid=(n // tm,),
        in_specs=[pl.BlockSpec((tm, d), row), _const_spec((1, d)), _const_spec(wdq.shape), _const_spec(qn.shape),
                  _const_spec(wqc.shape), _const_spec(wqs.shape), _const_spec(wdkv.shape), _const_spec(kvn.shape),
                  _const_spec(wuk.shape), _const_spec(wuv.shape),
                  pl.BlockSpec((tm, MLA_NOPE), pos), pl.BlockSpec((tm, MLA_NOPE), pos)],
        out_specs=[pl.BlockSpec((tm, heads * 2 * MLA_NOPE), row), pl.BlockSpec((tm, heads * 2 * MLA_NOPE), row),
                   pl.BlockSpec((tm, heads * 2 * MLA_V), row), pl.BlockSpec((tm, lora), row),
                   pl.BlockSpec((tm, MLA_ROPE), row)],
        out_shape=[jax.ShapeDtypeStruct((n, heads * 2 * MLA_NOPE), BF16),
                   jax.ShapeDtypeStruct((n, heads * 2 * MLA_NOPE), BF16),
                   jax.ShapeDtypeStruct((n, heads * 2 * MLA_V), BF16),
                   jax.ShapeDtypeStruct((n, lora), F32),
                   jax.ShapeDtypeStruct((n, MLA_ROPE), F32)],
        compiler_params=_cparams(1),
        name="mla_proj",
    )(x2d, g.reshape(1, d), wdq, qn, wqc, wqs, wdkv, kvn, wuk, wuv, cos_t, sin_t)


def _mla_expand_kernel(c_ref, r_ref, wuk_ref, wuv_ref, kc_ref, v_ref, *, heads):
    w = MLA_NOPE
    cb = c_ref[...].astype(BF16)
    kn = _dot(cb, wuk_ref[...])
    vv = _dot(cb, wuv_ref[...])
    krb = r_ref[...].astype(BF16)
    zeros = jnp.zeros((krb.shape[0], w - MLA_ROPE), BF16)
    ones = jnp.ones((krb.shape[0], w), BF16)
    for hd in range(heads):
        kc_ref[:, 2 * w * hd:2 * w * hd + w] = kn[:, w * hd:w * (hd + 1)].astype(BF16)
        kc_ref[:, 2 * w * hd + w:2 * w * hd + w + MLA_ROPE] = krb
        kc_ref[:, 2 * w * hd + w + MLA_ROPE:2 * w * (hd + 1)] = zeros
        v_ref[:, 2 * w * hd:2 * w * hd + w] = vv[:, w * hd:w * (hd + 1)].astype(BF16)
        v_ref[:, 2 * w * hd + w:2 * w * (hd + 1)] = ones


def _mla_expand(c2d, r2d, wuk, wuv, heads, tm):
    n, lora = c2d.shape
    row = lambda i: (i, 0)
    return pl.pallas_call(
        functools.partial(_mla_expand_kernel, heads=heads),
        grid=(n // tm,),
        in_specs=[pl.BlockSpec((tm, lora), row), pl.BlockSpec((tm, MLA_ROPE), row),
                  _const_spec(wuk.shape), _const_spec(wuv.shape)],
        out_specs=[pl.BlockSpec((tm, heads * 2 * MLA_NOPE), row), pl.BlockSpec((tm, heads * 2 * MLA_V), row)],
        out_shape=[jax.ShapeDtypeStruct((n, heads * 2 * MLA_NOPE), BF16),
                   jax.ShapeDtypeStruct((n, heads * 2 * MLA_V), BF16)],
        compiler_params=_cparams(1),
        name="mla_expand",
    )(c2d, r2d, wuk, wuv)


def _mla_attn_kernel(q_ref, kt_ref, v_ref, o_ref, m_ref, alpha_ref, p_ref, acc_ref, *, tq, tk, group, q_off, kv_len,
                     n_mask, mask_all):
    i = pl.program_id(2)
    q_start = q_off + i * tq
    q_chunk = (q_start + lax.broadcasted_iota(jnp.int32, (tq, 1), 0)) // CHUNK
    col = lax.broadcasted_iota(jnp.int32, (1, tk), 1)
    k_end = jnp.minimum(((q_start + tq - 1) // CHUNK + 1) * CHUNK, kv_len)
    n_blk = (k_end + tk - 1) // tk
    m_ref[...] = jnp.full(m_ref.shape, -jnp.inf, F32)
    acc_ref[...] = jnp.zeros(acc_ref.shape, F32)
    strip = min(tq, 64)

    n_rows = 2 if tq >= 256 else 1
    rh = tq // n_rows

    def process(blocks):
        starts = [pl.multiple_of(j * tk, tk) for j, _ in blocks]
        scores = [[_dot(q_ref[pl.ds(h * rh, rh), :], kt_ref[:, pl.ds(st, tk)]) for h in range(n_rows)]
                  for st in starts]
        for n, ((_, masked), st, s_halves) in enumerate(zip(blocks, starts, scores)):
            c = n % 2
            for r0 in range(0, tq, strip):
                rows = pl.ds(r0, strip)
                s = s_halves[r0 // rh][r0 % rh:r0 % rh + strip, :]
                if masked:
                    k_pos = st + col
                    s = jnp.where(((k_pos // CHUNK) <= q_chunk[r0:r0 + strip]) & (k_pos < kv_len), s, -jnp.inf)
                m_prev = m_ref[c, rows, :]
                m_new = jnp.maximum(m_prev, jnp.max(_lane_tiles(s, jnp.maximum), axis=-1, keepdims=True))
                p_ref[n, rows, :] = jnp.exp2(s - jnp.concatenate([m_new] * (tk // 128), axis=1)).astype(BF16)
                alpha_ref[n, rows, :] = jnp.exp2(m_prev - m_new)
                m_ref[c, rows, :] = m_new
        for n, st in enumerate(starts):
            c = n % 2
            for h in range(n_rows):
                rows = pl.ds(h * rh, rh)
                alpha = alpha_ref[n, rows, :]
                acc_ref[c, rows, :] = (jnp.concatenate([alpha, alpha], axis=1) * acc_ref[c, rows, :]
                                       + _dot(p_ref[n, rows, :], v_ref[pl.ds(st, tk), :]))

    def body(g, carry):
        process([(group * g + n, mask_all) for n in range(group)])
        return carry

    n_old = n_blk - n_mask
    n_groups = n_old // group
    lax.fori_loop(0, n_groups, body, 0)
    first = group * n_groups
    for rest in range(group):
        @pl.when(n_old - first == rest)
        def _(rest=rest):
            process([(first + n, mask_all) for n in range(rest)]
                    + [(first + rest + n, True) for n in range(n_mask)])

    m0, m1 = m_ref[0], m_ref[1]
    m = jnp.maximum(m0, m1)
    a0 = jnp.exp2(m0 - m)
    a1 = jnp.exp2(m1 - m)
    acc = jnp.concatenate([a0, a0], axis=1) * acc_ref[0] + jnp.concatenate([a1, a1], axis=1) * acc_ref[1]
    o_ref[...] = (acc[:, :MLA_V] / acc[:, MLA_V:]).astype(o_ref.dtype)


def _mla_attention(qc, kc, v1, heads, tq, tk, q_off, kv_len):
    b, t_q, _ = qc.shape
    t_k = kc.shape[1]
    w = 2 * MLA_NOPE
    mask_all = not (q_off == 0 and tq % tk == 0 and t_q == t_k == kv_len)
    n_mask = 1 if mask_all else tq // tk
    group = 4
    slots = group - 1 + n_mask
    kt = jnp.transpose(kc, (0, 2, 1))
    return pl.pallas_call(
        functools.partial(_mla_attn_kernel, tq=tq, tk=tk, group=group, q_off=q_off, kv_len=kv_len,
                          n_mask=n_mask, mask_all=mask_all),
        grid=(b, heads, t_q // tq),
        in_specs=[pl.BlockSpec((None, tq, w), lambda b_, h, i: (b_, i, h)),
                  pl.BlockSpec((None, w, t_k), lambda b_, h, i: (b_, h, 0)),
                  pl.BlockSpec((None, t_k, MLA_V + 128), lambda b_, h, i: (b_, 0, h))],
        out_specs=pl.BlockSpec((None, tq, MLA_V), lambda b_, h, i: (b_, i, h)),
        out_shape=jax.ShapeDtypeStruct((b, t_q, heads * MLA_V), BF16),
        scratch_shapes=[pltpu.VMEM((2, tq, 128), F32), pltpu.VMEM((slots, tq, 128), F32),
                        pltpu.VMEM((slots, tq, tk), BF16),
                        pltpu.VMEM((2, tq, MLA_V + 128), F32)],
        compiler_params=_cparams(3),
        name="mla_attention",
    )(qc, kt, v1)


def _rope_tables(pos):
    half = MLA_ROPE // 2
    inv_freq = ROPE_THETA ** (-jnp.arange(half, dtype=F32) / half)
    ang = pos.astype(F32)[:, None] * inv_freq[None, :]
    zeros = jnp.zeros((pos.shape[0], MLA_NOPE - MLA_ROPE), F32)
    cos, sin = jnp.cos(ang), jnp.sin(ang)
    return jnp.concatenate([cos, cos, zeros], axis=1), jnp.concatenate([sin, sin, zeros], axis=1)


def _swap_halves(w):
    half = w.shape[-1] // 2
    return jnp.concatenate([-w[..., half:], w[..., :half]], axis=-1)


def _mla_weights(w_dq, q_norm, w_uq, w_dkv, kv_norm, w_uk, w_uv, heads):
    q_lora = w_dq.shape[1]
    lora = w_uk.shape[0]
    pad = MLA_NOPE - MLA_ROPE
    uq = w_uq.reshape(q_lora, heads, MLA_NOPE + MLA_ROPE)
    zq = jnp.zeros((q_lora, heads, pad), F32)
    wqc = jnp.concatenate([uq, zq], axis=-1).reshape(q_lora, heads * 2 * MLA_NOPE)
    wqs = jnp.concatenate([_swap_halves(uq[..., MLA_NOPE:]), zq], axis=-1).reshape(q_lora, heads * MLA_NOPE)
    zk = jnp.zeros((w_dkv.shape[0], pad), F32)
    rope = w_dkv[:, lora:]
    wdkv = jnp.concatenate([w_dkv[:, :lora], rope, zk, _swap_halves(rope), zk], axis=1)
    return (w_dq.astype(BF16), q_norm.reshape(1, q_lora), wqc.astype(BF16), wqs.astype(BF16), wdkv.astype(BF16),
            kv_norm.reshape(1, lora), w_uk.reshape(lora, heads * MLA_NOPE).astype(BF16),
            w_uv.reshape(lora, heads * MLA_V).astype(BF16))


def _pad_rows(x, rows):
    return jnp.pad(x, ((0, 0), (0, rows - x.shape[1]), (0, 0)))


def kernel(x_prompt, x_sample, cache_sb_k, cache_sb_v, cache_band_k, cache_band_v, cache_mla_ckv, cache_mla_krope, norm_mix, norm_ffn, norm_final, w_gate_up, w_down, sb_w_qkv, sb_w_o, band_w_qkv, band_rel_bias, band_w_o, mla_w_dq, mla_q_norm, mla_w_uq, mla_w_dkv, mla_kv_norm, mla_w_uk, mla_w_uv, mla_w_o):
    bp, seq, d = x_prompt.shape
    bs, dec, _ = x_sample.shape
    depth = norm_mix.shape[0]
    past = cache_sb_k.shape[2]
    sb_heads, sb_dh = cache_sb_k.shape[3], cache_sb_k.shape[4]
    band_heads, band_dh = cache_band_k.shape[3], cache_band_k.shape[4]
    band_rows = cache_band_k.shape[2]
    mla_heads = mla_w_uk.shape[2]
    lora = mla_w_uk.shape[1]
    mla_scale = (MLA_NOPE + MLA_ROPE) ** -0.5 * LOG2E

    tm_p, tm_s = 512, bs * dec
    tm_ffn = 512
    blk = 256
    mla_blk = 512
    fc = 256

    xp = x_prompt.reshape(bp * seq, d)
    xs = x_sample.reshape(bs * dec, d)
    outs = {name: [] for name in ("sb_kp", "sb_vp", "sb_ks", "sb_vs", "band_kp", "band_vp", "band_ks", "band_vs",
                                  "mla_cp", "mla_rp", "mla_cs", "mla_rs")}

    for layer in range(depth):
        kind, j = layer % 3, layer // 3
        g_mix = norm_mix[layer]
        if kind == 0:
            w_bf = sb_w_qkv[j].astype(BF16)
            scale = sb_dh ** -0.5 * LOG2E
            q, k, v, kb, vb = _qkv_proj(xp, g_mix, w_bf, scale, tm_p)
            ap = _sb_attention(q.reshape(bp, seq, d), kb.reshape(bp, seq, d), vb.reshape(bp, seq, d),
                               sb_heads, blk, blk, 0)
            outs["sb_kp"].append(k.reshape(bp, seq, sb_heads, sb_dh))
            outs["sb_vp"].append(v.reshape(bp, seq, sb_heads, sb_dh))
            q, k, v, kb, vb = _qkv_proj(xs, g_mix, w_bf, scale, tm_s)
            t_k = -(-(past + dec) // blk) * blk
            k_all, v_all = _sb_keys(cache_sb_k, cache_sb_v, j, kb.reshape(bs, dec, d), vb.reshape(bs, dec, d), t_k)
            a_s = _sb_attention(q.reshape(bs, dec, d), k_all, v_all, sb_heads, dec, blk, past)
            outs["sb_ks"].append(k.reshape(bs, dec, sb_heads, sb_dh))
            outs["sb_vs"].append(v.reshape(bs, dec, sb_heads, sb_dh))
            w_o = sb_w_o[j]
        elif kind == 1:
            w_bf = band_w_qkv[j].astype(BF16)
            scale = band_dh ** -0.5 * LOG2E
            win = BAND_CHUNKS * CHUNK
            q, k, v, kb, vb = _qkv_proj(xp, g_mix, w_bf, scale, tm_p, ones_v=True)
            tile = _band_tile(band_rel_bias[j], win, blk, 0, 3 * blk, 3 * blk) * LOG2E
            ap = _band_attention(q.reshape(bp, seq, d), kb.reshape(bp, seq, d), vb.reshape(bp, seq, 2 * d), tile,
                                 band_heads, blk, blk, True)
            keep = min(win, seq)
            outs["band_kp"].append(k.reshape(bp, seq, d)[:, seq - keep:].reshape(bp, keep, band_heads, band_dh))
            outs["band_vp"].append(v.reshape(bp, seq, d)[:, seq - keep:].reshape(bp, keep, band_heads, band_dh))
            q, k, v, kb, vb = _qkv_proj(xs, g_mix, w_bf, scale, tm_s, ones_v=True)
            n_keys = band_rows + dec
            k_all = _pad_rows(jnp.concatenate([cache_band_k[j].reshape(bs, band_rows, d).astype(BF16),
                                               kb.reshape(bs, dec, d)], axis=1), 3 * blk)
            cache_v = cache_band_v[j].reshape(bs, band_rows, d // 128, 128).astype(BF16)
            cache_v = jnp.concatenate([cache_v, jnp.ones_like(cache_v)], axis=-1).reshape(bs, band_rows, 2 * d)
            v_all = _pad_rows(jnp.concatenate([cache_v, vb.reshape(bs, dec, 2 * d)], axis=1), 3 * blk)
            tile = _band_tile(band_rel_bias[j], past, dec, past - band_rows, 3 * blk, n_keys) * LOG2E
            a_s = _band_attention(q.reshape(bs, dec, d), k_all, v_all, tile, band_heads, dec, blk, False)
            outs["band_ks"].append(k.reshape(bs, dec, band_heads, band_dh))
            outs["band_vs"].append(v.reshape(bs, dec, band_heads, band_dh))
            w_o = band_w_o[j]
        else:
            wts = _mla_weights(mla_w_dq[j], mla_q_norm[j], mla_w_uq[j], mla_w_dkv[j], mla_kv_norm[j],
                               mla_w_uk[j], mla_w_uv[j], mla_heads)
            cos_p, sin_p = _rope_tables(jnp.arange(seq))
            qc, kc, vv, ckv, kr = _mla_proj(xp, g_mix, wts, cos_p, sin_p, mla_heads, mla_scale, tm_p)
            ap = _mla_attention(qc.reshape(bp, seq, -1), kc.reshape(bp, seq, -1), vv.reshape(bp, seq, -1),
                                mla_heads, mla_blk, mla_blk, 0, seq)
            outs["mla_cp"].append(ckv.reshape(bp, seq, lora))
            outs["mla_rp"].append(kr.reshape(bp, seq, MLA_ROPE))
            cos_s, sin_s = _rope_tables(jnp.tile(past + jnp.arange(dec), bs))
            qc, kc, vv, ckv, kr = _mla_proj(xs, g_mix, wts, cos_s, sin_s, mla_heads, mla_scale, tm_s)
            kc_c, v_c = _mla_expand(cache_mla_ckv[j].reshape(bs * past, lora),
                                    cache_mla_krope[j].reshape(bs * past, MLA_ROPE), wts[6], wts[7], mla_heads, blk)
            t_k = -(-(past + dec) // blk) * blk
            kc_all = _pad_rows(jnp.concatenate([kc_c.reshape(bs, past, -1), kc.reshape(bs, dec, -1)], axis=1), t_k)
            v_all = _pad_rows(jnp.concatenate([v_c.reshape(bs, past, -1), vv.reshape(bs, dec, -1)], axis=1), t_k)
            a_s = _mla_attention(qc.reshape(bs, dec, -1), kc_all, v_all, mla_heads, dec, blk, past, past + dec)
            outs["mla_cs"].append(ckv.reshape(bs, dec, lora))
            outs["mla_rs"].append(kr.reshape(bs, dec, MLA_ROPE))
            w_o = mla_w_o[j]

        wgu, wd = w_gate_up[layer].astype(BF16), w_down[layer].astype(BF16)
        final = layer == depth - 1
        wo_bf = w_o.astype(BF16)
        xp = _oproj_ffn(xp, ap.reshape(bp * seq, d), wo_bf, norm_ffn[layer], wgu, wd, norm_final, final, tm_ffn, fc)
        xs = _oproj_ffn(xs, a_s.reshape(bs * dec, d), wo_bf, norm_ffn[layer], wgu, wd, norm_final, final, tm_s, fc)

    st = lambda name: jnp.stack(outs[name])
    return (xp.reshape(bp, seq, d), xs.reshape(bs, dec, d),
            st("sb_kp"), st("sb_vp"), st("band_kp"), st("band_vp"), st("mla_cp"), st("mla_rp"),
            st("sb_ks"), st("sb_vs"), st("band_ks"), st("band_vs"), st("mla_cs"), st("mla_rs"))
```

```python
import functools

import jax
import jax.numpy as jnp
from jax import lax
from jax.experimental import pallas as pl
from jax.experimental.pallas import tpu as pltpu

CHUNK = 64
BAND_CHUNKS = 8
REL_MAX = 128
MLA_NOPE = 128
MLA_ROPE = 64
MLA_V = 128
ROPE_THETA = 10000.0
EPS = 1e-6

LOG2E = 1.4426950408889634
SB_LOG2_ZERO = -151.0

VMEM_LIMIT = 56 * 1024 * 1024
F32 = jnp.float32
BF16 = jnp.bfloat16


def _cparams(n_grid):
    return pltpu.CompilerParams(dimension_semantics=("arbitrary",) * n_grid, vmem_limit_bytes=VMEM_LIMIT)


def _rms(x, g):
    return x * lax.rsqrt(jnp.mean(x * x, axis=-1, keepdims=True) + EPS) * g


def _dot(a, b):
    return jnp.dot(a, b, preferred_element_type=F32)


def _dot_nt(a, b):
    return lax.dot_general(a, b, (((1,), (1,)), ((), ())), preferred_element_type=F32)


def _lane_tiles(x, op):
    tiles = [x[:, t:t + 128] for t in range(0, x.shape[1], 128)]
    while len(tiles) > 1:
        tiles = [op(tiles[t], tiles[t + 1]) for t in range(0, len(tiles) - 1, 2)] + tiles[len(tiles) & ~1:]
    return tiles[0]


def _const_spec(shape):
    nd = len(shape)
    return pl.BlockSpec(shape, lambda *_: (0,) * nd)


def _qkv_kernel(x_ref, g_ref, w_ref, q_ref, k_ref, v_ref, kb_ref, vb_ref, *, d, q_scale, ones_v):
    h = _rms(x_ref[...], g_ref[...]).astype(BF16)
    q = _dot(h, w_ref[:, 0:d])
    q_ref[...] = (q * q_scale).astype(BF16)
    k = _dot(h, w_ref[:, d:2 * d])
    k_ref[...] = k
    kb_ref[...] = k.astype(BF16)
    v = _dot(h, w_ref[:, 2 * d:3 * d])
    v_ref[...] = v
    if ones_v:
        ones = jnp.ones((k.shape[0], 128), BF16)
        for t in range(d // 128):
            vb_ref[:, 256 * t:256 * t + 128] = v[:, 128 * t:128 * (t + 1)].astype(BF16)
            vb_ref[:, 256 * t + 128:256 * (t + 1)] = ones
    else:
        vb_ref[...] = v.astype(BF16)


def _qkv_proj(x2d, g, w_bf, q_scale, tm, ones_v=False):
    n, d = x2d.shape
    row = lambda i: (i, 0)
    wv = 2 * d if ones_v else d
    return pl.pallas_call(
        functools.partial(_qkv_kernel, d=d, q_scale=q_scale, ones_v=ones_v),
        grid=(n // tm,),
        in_specs=[pl.BlockSpec((tm, d), row), _const_spec((1, d)), _const_spec((d, 3 * d))],
        out_specs=[pl.BlockSpec((tm, d), row)] * 4 + [pl.BlockSpec((tm, wv), row)],
        out_shape=[jax.ShapeDtypeStruct((n, d), BF16), jax.ShapeDtypeStruct((n, d), F32),
                   jax.ShapeDtypeStruct((n, d), F32), jax.ShapeDtypeStruct((n, d), BF16),
                   jax.ShapeDtypeStruct((n, wv), BF16)],
        compiler_params=_cparams(1),
        name="qkv_proj",
    )(x2d, g.reshape(1, d), w_bf)


def _ffn_kernel(x_ref, a_ref, wo_ref, g_ref, wgu_ref, wd_ref, gf_ref, o_ref, acc_ref, h_ref, *, n_chunks, fc, final):
    x1 = x_ref[...] + _dot(a_ref[...], wo_ref[...])
    h_ref[...] = _rms(x1, g_ref[...]).astype(BF16)
    acc_ref[...] = x1

    def body(c, carry):
        lo = pl.multiple_of(c * fc, fc)
        gate = _dot(h_ref[...], wgu_ref[:, pl.ds(lo, fc)])
        up = _dot(h_ref[...], wgu_ref[:, pl.ds(pl.multiple_of(n_chunks * fc + lo, fc), fc)])
        act = (gate / (1.0 + jnp.exp(-gate)) * up).astype(BF16)
        acc_ref[...] += _dot(act, wd_ref[pl.ds(lo, fc), :])
        return carry

    lax.fori_loop(0, n_chunks, body, 0, unroll=True)
    out = acc_ref[...]
    if final:
        out = _rms(out, gf_ref[...])
    o_ref[...] = out


def _oproj_ffn(x2d, a2d, wo_bf, g, wgu_c, wd_c, g_final, final, tm, fc):
    n, d = x2d.shape
    n_chunks = wd_c.shape[0] // fc
    row = lambda i: (i, 0)
    return pl.pallas_call(
        functools.partial(_ffn_kernel, n_chunks=n_chunks, fc=fc, final=final),
        grid=(n // tm,),
        in_specs=[pl.BlockSpec((tm, d), row), pl.BlockSpec((tm, d), row), _const_spec((d, d)), _const_spec((1, d)),
                  _const_spec(wgu_c.shape), _const_spec(wd_c.shape), _const_spec((1, d))],
        out_specs=pl.BlockSpec((tm, d), row),
        out_shape=jax.ShapeDtypeStruct((n, d), F32),
        scratch_shapes=[pltpu.VMEM((tm, d), F32), pltpu.VMEM((tm, d), BF16)],
        compiler_params=_cparams(1),
        name="oproj_ffn",
    )(x2d, a2d, wo_bf, g.reshape(1, d), wgu_c, wd_c, g_final.reshape(1, d))


def _sb_kernel(q_ref, k_ref, v_ref, tri_ref, o_ref, a_ref, acc_ref, *, tq, tk, dh, n_heads, q_off):
    i = pl.program_id(2)
    q_start = q_off + i * tq
    q_pos = q_start + lax.broadcasted_iota(jnp.int32, (tq, 1), 0)
    col = lax.broadcasted_iota(jnp.int32, (1, tk), 1)
    j_last = (q_start + tq - 2) // tk
    a_ref[...] = jnp.zeros(a_ref.shape, F32)
    acc_ref[...] = jnp.zeros(acc_ref.shape, F32)

    def cond(state):
        j, a_max = state
        return jnp.logical_and(j >= 0, a_max > SB_LOG2_ZERO)

    def body(state):
        j, _ = state
        tri = tri_ref[...]
        lanes = [pl.ds(c * dh, dh) for c in range(n_heads)]
        starts = [pl.multiple_of(j * tk, tk), pl.multiple_of(jnp.maximum(j - 1, 0) * tk, tk)]
        masks = [(starts[0] + col) < q_pos, jnp.logical_and((starts[1] + col) < q_pos, j >= 1)]
        zs = [[jnp.where(mask, _dot_nt(q_ref[:, ln], k_ref[pl.ds(st, tk), ln]), -jnp.inf) for ln in lanes]
              for st, mask in zip(starts, masks)]
        softs = [[jnp.maximum(z, 0.0) + jnp.log(1.0 + jnp.exp2(-jnp.abs(z))) * LOG2E for z in zb] for zb in zs]
        rs = []
        for sb in softs:
            rb = []
            for soft in sb:
                rb.append(_dot(soft.astype(BF16), tri))
            rs.append(rb)
        a_max = jnp.float32(-jnp.inf)
        ws = [[], []]
        for c in range(n_heads):
            a = a_ref[c]
            for blk in range(2):
                a_wide = jnp.concatenate([a] * (tk // 128), axis=1)
                ws[blk].append(jnp.exp2(zs[blk][c] + rs[blk][c] + a_wide).astype(BF16))
                a = a - jnp.sum(_lane_tiles(softs[blk][c], jnp.add), axis=-1, keepdims=True)
            a_ref[c] = a
            a_max = jnp.maximum(a_max, jnp.max(a))
        for c in range(n_heads):
            acc_ref[:, lanes[c]] += (_dot(ws[0][c], v_ref[pl.ds(starts[0], tk), lanes[c]])
                                     + _dot(ws[1][c], v_ref[pl.ds(starts[1], tk), lanes[c]]))
        return j - 2, a_max

    lax.while_loop(cond, body, (j_last, jnp.float32(0.0)))
    o_ref[...] = acc_ref[...].astype(o_ref.dtype)


def _sb_keys_kernel(ck_ref, cv_ref, nk_ref, nv_ref, k_ref, v_ref, *, heads, past, dec):
    dh = ck_ref.shape[2]
    for c_ref, n_ref, o_ref in ((ck_ref, nk_ref, k_ref), (cv_ref, nv_ref, v_ref)):
        by_head = pltpu.einshape("rhd->hrd", c_ref[...])
        for hd in range(heads):
            o_ref[0:past, hd * dh:(hd + 1) * dh] = by_head[hd].astype(BF16)
        o_ref[past:past + dec, :] = n_ref[...]
        o_ref[past + dec:, :] = jnp.zeros((o_ref.shape[0] - past - dec, o_ref.shape[1]), BF16)


def _sb_keys(cache_k, cache_v, layer, new_k, new_v, t_k):
    _, b, past, heads, dh = cache_k.shape
    dec, d = new_k.shape[1], new_k.shape[2]
    cache_spec = pl.BlockSpec((None, None, past, heads, dh), lambda i: (layer, i, 0, 0, 0))
    new_spec = pl.BlockSpec((None, dec, d), lambda i: (i, 0, 0))
    out_spec = pl.BlockSpec((None, t_k, d), lambda i: (i, 0, 0))
    return pl.pallas_call(
        functools.partial(_sb_keys_kernel, heads=heads, past=past, dec=dec),
        grid=(b,),
        in_specs=[cache_spec, cache_spec, new_spec, new_spec],
        out_specs=[out_spec, out_spec],
        out_shape=[jax.ShapeDtypeStruct((b, t_k, d), BF16)] * 2,
        compiler_params=_cparams(1),
        name="sb_keys",
    )(cache_k, cache_v, new_k, new_v)


def _sb_attention(q, k, v, heads, tq, tk, q_off):
    b, t_q, d = q.shape
    t_k = k.shape[1]
    dh = d // heads
    n_heads = 2
    tri = -jnp.tril(jnp.ones((tk, tk), F32)).astype(BF16)
    return pl.pallas_call(
        functools.partial(_sb_kernel, tq=tq, tk=tk, dh=dh, n_heads=n_heads, q_off=q_off),
        grid=(b, heads // n_heads, t_q // tq),
        in_specs=[pl.BlockSpec((None, tq, n_heads * dh), lambda b_, h, i: (b_, i, h)),
                  pl.BlockSpec((None, t_k, n_heads * dh), lambda b_, h, i: (b_, 0, h)),
                  pl.BlockSpec((None, t_k, n_heads * dh), lambda b_, h, i: (b_, 0, h)),
                  _const_spec((tk, tk))],
        out_specs=pl.BlockSpec((None, tq, n_heads * dh), lambda b_, h, i: (b_, i, h)),
        out_shape=jax.ShapeDtypeStruct((b, t_q, d), BF16),
        scratch_shapes=[pltpu.VMEM((n_heads, tq, 128), F32), pltpu.VMEM((tq, n_heads * dh), F32)],
        compiler_params=_cparams(3),
        name="sb_attention",
    )(q, k, v, tri)


def _band_kernel(q_ref, k0_ref, k1_ref, k2_ref, v0_ref, v1_ref, v2_ref, tile_ref, o_ref, *, tk, pairs, prompt):
    tq = q_ref.shape[0]
    pair = 128
    half = pair // 2
    lane = lax.broadcasted_iota(jnp.int32, (1, pair), 1)
    col = lax.broadcasted_iota(jnp.int32, (1, 3 * tk), 1)

    def compute(first_col):
        scores = []
        for c in range(pairs):
            ln = pl.ds(c * pair, pair)
            q = q_ref[:, ln]
            kw = jnp.concatenate([k0_ref[:, ln], k1_ref[:, ln], k2_ref[:, ln]], axis=0)
            for hh in range(2):
                in_head = (lane >= hh * half) & (lane < (hh + 1) * half)
                s = _dot_nt(jnp.where(in_head, q, jnp.zeros_like(q)), kw) + tile_ref[2 * c + hh]
                if first_col is not None:
                    s = jnp.where(col >= first_col, s, -jnp.inf)
                scores.append(s)
        weights = []
        for s in scores:
            m = jnp.broadcast_to(jnp.max(_lane_tiles(s, jnp.maximum), axis=-1, keepdims=True), (tq, 128))
            weights.append(jnp.exp2(s - jnp.concatenate([m] * (3 * tk // 128), axis=1)).astype(BF16))
        for c in range(pairs):
            lv = pl.ds(c * 2 * pair, 2 * pair)
            vw = jnp.concatenate([v0_ref[:, lv], v1_ref[:, lv], v2_ref[:, lv]], axis=0)
            outs = []
            for hh in range(2):
                r = _dot(weights[2 * c + hh], vw)
                outs.append(r[:, :pair] / r[:, pair:])
            o_ref[:, pl.ds(c * pair, pair)] = jnp.where(lane < half, outs[0], outs[1]).astype(o_ref.dtype)

    if prompt:
        i = pl.program_id(2)

        @pl.when(i < 2)
        def _():
            compute((2 - i) * tk)

        @pl.when(i >= 2)
        def _():
            compute(None)
    else:
        compute(None)


def _band_tile(rel_bias, q0, n_q, k0, n_k, n_valid):
    heads = rel_bias.shape[0]
    n = n_q + n_k - 1
    dist = (q0 - k0) + (n_q - 1) - jnp.arange(n)
    g = rel_bias[:, jnp.clip(dist, -REL_MAX, REL_MAX) + REL_MAX].astype(F32)
    flat = jnp.tile(jnp.pad(g, ((0, 0), (0, 1))), (1, n_q))[:, :n_q * n]
    bias = flat.reshape(heads, n_q, n)[:, :, n_q - 1:]
    q_chunk = ((q0 + jnp.arange(n_q)) // CHUNK)[:, None]
    k_chunk = ((k0 + jnp.arange(n_k)) // CHUNK)[None, :]
    mask = (jnp.arange(n_k) < n_valid)[None, :] & (k_chunk <= q_chunk) & (k_chunk >= q_chunk - BAND_CHUNKS)
    return jnp.where(mask[None], bias, -jnp.inf)


def _band_attention(q, k, v, tile, heads, tq, tk, prompt):
    b, t_q, d = q.shape
    assert 2 * (d // heads) == 128
    pairs = 2
    w = 128 * pairs
    if prompt:
        kmap = lambda t: (lambda hp, b_, i: (b_, jnp.maximum(i - 2 + t, 0), hp))
    else:
        kmap = lambda t: (lambda hp, b_, i: (b_, t, hp))
    kspecs = [pl.BlockSpec((None, tk, w), kmap(t)) for t in range(3)]
    vspecs = [pl.BlockSpec((None, tk, 2 * w), kmap(t)) for t in range(3)]
    return pl.pallas_call(
        functools.partial(_band_kernel, tk=tk, pairs=pairs, prompt=prompt),
        grid=(heads // (2 * pairs), b, t_q // tq),
        in_specs=[pl.BlockSpec((None, tq, w), lambda hp, b_, i: (b_, i, hp))] + kspecs + vspecs
        + [pl.BlockSpec((2 * pairs, tq, 3 * tk), lambda hp, b_, i: (hp, 0, 0))],
        out_specs=pl.BlockSpec((None, tq, w), lambda hp, b_, i: (b_, i, hp)),
        out_shape=jax.ShapeDtypeStruct((b, t_q, d), BF16),
        compiler_params=_cparams(3),
        name="band_attention",
    )(q, k, k, k, v, v, v, tile)


def _mla_proj_kernel(x_ref, g_ref, wdq_ref, qn_ref, wqc_ref, wqs_ref, wdkv_ref, kvn_ref, wuk_ref, wuv_ref,
                     cos_ref, sin_ref, qc_ref, kc_ref, v_ref, ckv_ref, kr_ref, *, heads, scale):
    w = MLA_NOPE
    h = _rms(x_ref[...], g_ref[...]).astype(BF16)
    qn = _rms(_dot(h, wdq_ref[...]), qn_ref[...]).astype(BF16)
    a = _dot(qn, wqc_ref[...])
    sw = _dot(qn, wqs_ref[...])
    cos = cos_ref[...]
    sin = sin_ref[...]
    for hd in range(heads):
        qc_ref[:, 2 * w * hd:2 * w * hd + w] = (a[:, 2 * w * hd:2 * w * hd + w] * scale).astype(BF16)
        rope = a[:, 2 * w * hd + w:2 * w * (hd + 1)] * cos + sw[:, w * hd:w * (hd + 1)] * sin
        qc_ref[:, 2 * w * hd + w:2 * w * (hd + 1)] = (rope * scale).astype(BF16)
    kv = _dot(h, wdkv_ref[...])
    lora = kv.shape[1] - 2 * w
    ckv = _rms(kv[:, :lora], kvn_ref[...])
    ckv_ref[...] = ckv
    kr = kv[:, lora:lora + w] * cos + kv[:, lora + w:lora + 2 * w] * sin
    kr_ref[...] = kr[:, :MLA_ROPE]
    cb = ckv.astype(BF16)
    kn = _dot(cb, wuk_ref[...])
    vv = _dot(cb, wuv_ref[...])
    krb = kr.astype(BF16)
    ones = jnp.ones((krb.shape[0], w), BF16)
    for hd in range(heads):
        kc_ref[:, 2 * w * hd:2 * w * hd + w] = kn[:, w * hd:w * (hd + 1)].astype(BF16)
        kc_ref[:, 2 * w * hd + w:2 * w * (hd + 1)] = krb
        v_ref[:, 2 * w * hd:2 * w * hd + w] = vv[:, w * hd:w * (hd + 1)].astype(BF16)
        v_ref[:, 2 * w * hd + w:2 * w * (hd + 1)] = ones


def _mla_proj(x2d, g, wts, cos_t, sin_t, heads, scale, tm):
    n, d = x2d.shape
    wdq, qn, wqc, wqs, wdkv, kvn, wuk, wuv = wts
    lora = wuk.shape[0]
    n_pos = cos_t.shape[0] // tm
    row = lambda i: (i, 0)
    pos = lambda i: (i % n_pos, 0)
    return pl.pallas_call(
        functools.partial(_mla_proj_kernel, heads=heads, scale=scale),
        grid=(n // tm,),
        in_specs=[pl.BlockSpec((tm, d), row), _const_spec((1, d)), _const_spec(wdq.shape), _const_spec(qn.shape),
                  _const_spec(wqc.shape), _const_spec(wqs.shape), _const_spec(wdkv.shape), _const_spec(kvn.shape),
                  _const_spec(wuk.shape), _const_spec(wuv.shape),
                  pl.BlockSpec((tm, MLA_NOPE), pos), pl.BlockSpec((tm, MLA_NOPE), pos)],
        out_specs=[pl.BlockSpec((tm, heads * 2 * MLA_NOPE), row), pl.BlockSpec((tm, heads * 2 * MLA_NOPE), row),
                   pl.BlockSpec((tm, heads * 2 * MLA_V), row), pl.BlockSpec((tm, lora), row),
                   pl.BlockSpec((tm, MLA_ROPE), row)],
        out_shape=[jax.ShapeDtypeStruct((n, heads * 2 * MLA_NOPE), BF16),
                   jax.ShapeDtypeStruct((n, heads * 2 * MLA_NOPE), BF16),
                   jax.ShapeDtypeStruct((n, heads * 2 * MLA_V), BF16),
                   jax.ShapeDtypeStruct((n, lora), F32),
                   jax.ShapeDtypeStruct((n, MLA_ROPE), F32)],
        compiler_params=_cparams(1),
        name="mla_proj",
    )(x2d, g.reshape(1, d), wdq, qn, wqc, wqs, wdkv, kvn, wuk, wuv, cos_t, sin_t)


def _mla_attn_kernel(q_ref, kt_ref, v_ref, o_ref, m_ref, alpha_ref, p_ref, acc_ref, *, tq, tk, group, q_off, kv_len,
                     n_mask, mask_all):
    i = pl.program_id(2)
    q_start = q_off + i * tq
    q_chunk = (q_start + lax.broadcasted_iota(jnp.int32, (tq, 1), 0)) // CHUNK
    col = lax.broadcasted_iota(jnp.int32, (1, tk), 1)
    k_end = jnp.minimum(((q_start + tq - 1) // CHUNK + 1) * CHUNK, kv_len)
    n_blk = (k_end + tk - 1) // tk
    m_ref[...] = jnp.full(m_ref.shape, -jnp.inf, F32)
    acc_ref[...] = jnp.zeros(acc_ref.shape, F32)
    strip = min(tq, 64)

    n_rows = 2 if tq >= 256 else 1
    rh = tq // n_rows

    def process(blocks):
        starts = [pl.multiple_of(j * tk, tk) for j, _ in blocks]
        scores = [[_dot(q_ref[pl.ds(h * rh, rh), :], kt_ref[:, pl.ds(st, tk)]) for h in range(n_rows)]
                  for st in starts]
        for n, ((_, masked), st, s_halves) in enumerate(zip(blocks, starts, scores)):
            c = n % 2
            for r0 in range(0, tq, strip):
                rows = pl.ds(r0, strip)
                s = s_halves[r0 // rh][r0 % rh:r0 % rh + strip, :]
                if masked:
                    k_pos = st + col
                    s = jnp.where(((k_pos // CHUNK) <= q_chunk[r0:r0 + strip]) & (k_pos < kv_len), s, -jnp.inf)
                m_prev = m_ref[c, rows, :]
                m_new = jnp.maximum(m_prev, jnp.max(_lane_tiles(s, jnp.maximum), axis=-1, keepdims=True))
                p_ref[n, rows, :] = jnp.exp2(s - jnp.concatenate([m_new] * (tk // 128), axis=1)).astype(BF16)
                alpha_ref[n, rows, :] = jnp.exp2(m_prev - m_new)
                m_ref[c, rows, :] = m_new
        for n, st in enumerate(starts):
            c = n % 2
            for h in range(n_rows):
                rows = pl.ds(h * rh, rh)
                alpha = alpha_ref[n, rows, :]
                acc_ref[c, rows, :] = (jnp.concatenate([alpha, alpha], axis=1) * acc_ref[c, rows, :]
                                       + _dot(p_ref[n, rows, :], v_ref[pl.ds(st, tk), :]))

    def body(g, carry):
        process([(group * g + n, mask_all) for n in range(group)])
        return carry

    n_old = n_blk - n_mask
    n_groups = n_old // group
    lax.fori_loop(0, n_groups, body, 0)
    first = group * n_groups
    for rest in range(group):
        @pl.when(n_old - first == rest)
        def _(rest=rest):
            process([(first + n, mask_all) for n in range(rest)]
                    + [(first + rest + n, True) for n in range(n_mask)])

    m0, m1 = m_ref[0], m_ref[1]
    m = jnp.maximum(m0, m1)
    a0 = jnp.exp2(m0 - m)
    a1 = jnp.exp2(m1 - m)
    acc = jnp.concatenate([a0, a0], axis=1) * acc_ref[0] + jnp.concatenate([a1, a1], axis=1) * acc_ref[1]
    o_ref[...] = (acc[:, :MLA_V] / acc[:, MLA_V:]).astype(o_ref.dtype)


def _mla_attention(qc, kc, v1, heads, tq, tk, q_off, kv_len, group=4):
    b, t_q, _ = qc.shape
    t_k = kc.shape[1]
    w = 2 * MLA_NOPE
    mask_all = not (q_off == 0 and (tq % tk == 0 or tk % tq == 0) and t_q == t_k == kv_len)
    n_mask = 1 if mask_all else max(1, tq // tk)
    slots = group - 1 + n_mask
    kt = jnp.transpose(kc, (0, 2, 1))
    return pl.pallas_call(
        functools.partial(_mla_attn_kernel, tq=tq, tk=tk, group=group, q_off=q_off, kv_len=kv_len,
                          n_mask=n_mask, mask_all=mask_all),
        grid=(b, heads, t_q // tq),
        in_specs=[pl.BlockSpec((None, tq, w), lambda b_, h, i: (b_, i, h)),
                  pl.BlockSpec((None, w, t_k), lambda b_, h, i: (b_, h, 0)),
                  pl.BlockSpec((None, t_k, MLA_V + 128), lambda b_, h, i: (b_, 0, h))],
        out_specs=pl.BlockSpec((None, tq, MLA_V), lambda b_, h, i: (b_, i, h)),
        out_shape=jax.ShapeDtypeStruct((b, t_q, heads * MLA_V), BF16),
        scratch_shapes=[pltpu.VMEM((2, tq, 128), F32), pltpu.VMEM((slots, tq, 128), F32),
                        pltpu.VMEM((slots, tq, tk), BF16),
                        pltpu.VMEM((2, tq, MLA_V + 128), F32)],
        compiler_params=_cparams(3),
        name="mla_attention",
    )(qc, kt, v1)


def _mla_sample_kernel(q_ref, cc_ref, cr_ref, nc_ref, nr_ref, wuk_ref, wuv_ref, o_ref, c_all, r_all, *,
                       heads, past, dec):
    w = MLA_NOPE
    t_k = c_all.shape[0]
    c_all[0:past, :] = cc_ref[...].astype(BF16)
    c_all[past:past + dec, :] = nc_ref[...].astype(BF16)
    c_all[past + dec:, :] = jnp.zeros((t_k - past - dec, c_all.shape[1]), BF16)
    r_all[0:past, :] = cr_ref[...].astype(BF16)
    r_all[past:past + dec, :] = nr_ref[...].astype(BF16)
    r_all[past + dec:, :] = jnp.zeros((t_k - past - dec, r_all.shape[1]), BF16)
    q_lat = jnp.concatenate([_dot_nt(q_ref[:, 2 * w * hd:2 * w * hd + w],
                                     wuk_ref[:, w * hd:w * (hd + 1)]).astype(BF16) for hd in range(heads)], axis=0)
    q_rope = jnp.concatenate([q_ref[:, 2 * w * hd + w:2 * w * hd + w + MLA_ROPE] for hd in range(heads)], axis=0)
    k_pos = lax.broadcasted_iota(jnp.int32, (1, t_k), 1)
    q_chunk = (past + lax.broadcasted_iota(jnp.int32, (heads * dec, 1), 0) % dec) // CHUNK
    s = _dot_nt(q_lat, c_all[...]) + _dot_nt(q_rope, r_all[...])
    s = jnp.where(((k_pos // CHUNK) <= q_chunk) & (k_pos < past + dec), s, -jnp.inf)
    p = jnp.exp2(s - jnp.max(s, axis=-1, keepdims=True))
    o_lat = (_dot(p.astype(BF16), c_all[...]) / jnp.sum(p, axis=-1, keepdims=True)).astype(BF16)
    for hd in range(heads):
        o_ref[:, w * hd:w * (hd + 1)] = _dot(o_lat[hd * dec:(hd + 1) * dec, :],
                                             wuv_ref[:, w * hd:w * (hd + 1)]).astype(o_ref.dtype)


def _mla_sample_attention(qc, cache_ckv, cache_krope, layer, ckv_new, kr_new, wuk, wuv, heads):
    b, dec, _ = qc.shape
    past, lora = cache_ckv.shape[2], cache_ckv.shape[3]
    t_k = -(-(past + dec) // 128) * 128
    cache = lambda width: pl.BlockSpec((None, None, past, width), lambda i: (layer, i, 0, 0))
    new = lambda width: pl.BlockSpec((None, dec, width), lambda i: (i, 0, 0))
    return pl.pallas_call(
        functools.partial(_mla_sample_kernel, heads=heads, past=past, dec=dec),
        grid=(b,),
        in_specs=[new(qc.shape[2]), cache(lora), cache(MLA_ROPE), new(lora), new(MLA_ROPE),
                  _const_spec(wuk.shape), _const_spec(wuv.shape)],
        out_specs=new(heads * MLA_V),
        out_shape=jax.ShapeDtypeStruct((b, dec, heads * MLA_V), BF16),
        scratch_shapes=[pltpu.VMEM((t_k, lora), BF16), pltpu.VMEM((t_k, MLA_ROPE), BF16)],
        compiler_params=_cparams(1),
        name="mla_sample_attention",
    )(qc, cache_ckv, cache_krope, ckv_new, kr_new, wuk, wuv)


def _rope_tables(pos):
    half = MLA_ROPE // 2
    inv_freq = ROPE_THETA ** (-jnp.arange(half, dtype=F32) / half)
    ang = pos.astype(F32)[:, None] * inv_freq[None, :]
    zeros = jnp.zeros((pos.shape[0], MLA_NOPE - MLA_ROPE), F32)
    cos, sin = jnp.cos(ang), jnp.sin(ang)
    return jnp.concatenate([cos, cos, zeros], axis=1), jnp.concatenate([sin, sin, zeros], axis=1)


def _swap_halves(w):
    half = w.shape[-1] // 2
    return jnp.concatenate([-w[..., half:], w[..., :half]], axis=-1)


def _mla_weights(w_dq, q_norm, w_uq, w_dkv, kv_norm, w_uk, w_uv, heads):
    q_lora = w_dq.shape[1]
    lora = w_uk.shape[0]
    pad = MLA_NOPE - MLA_ROPE
    uq = w_uq.reshape(q_lora, heads, MLA_NOPE + MLA_ROPE)
    zq = jnp.zeros((q_lora, heads, pad), F32)
    wqc = jnp.concatenate([uq, zq], axis=-1).reshape(q_lora, heads * 2 * MLA_NOPE)
    wqs = jnp.concatenate([_swap_halves(uq[..., MLA_NOPE:]), zq], axis=-1).reshape(q_lora, heads * MLA_NOPE)
    zk = jnp.zeros((w_dkv.shape[0], pad), F32)
    rope = w_dkv[:, lora:]
    wdkv = jnp.concatenate([w_dkv[:, :lora], rope, zk, _swap_halves(rope), zk], axis=1)
    return (w_dq.astype(BF16), q_norm.reshape(1, q_lora), wqc.astype(BF16), wqs.astype(BF16), wdkv.astype(BF16),
            kv_norm.reshape(1, lora), w_uk.reshape(lora, heads * MLA_NOPE).astype(BF16),
            w_uv.reshape(lora, heads * MLA_V).astype(BF16))


def _pad_rows(x, rows):
    return jnp.pad(x, ((0, 0), (0, rows - x.shape[1]), (0, 0)))


def kernel(x_prompt, x_sample, cache_sb_k, cache_sb_v, cache_band_k, cache_band_v, cache_mla_ckv, cache_mla_krope, norm_mix, norm_ffn, norm_final, w_gate_up, w_down, sb_w_qkv, sb_w_o, band_w_qkv, band_rel_bias, band_w_o, mla_w_dq, mla_q_norm, mla_w_uq, mla_w_dkv, mla_kv_norm, mla_w_uk, mla_w_uv, mla_w_o):
    bp, seq, d = x_prompt.shape
    bs, dec, _ = x_sample.shape
    depth = norm_mix.shape[0]
    past = cache_sb_k.shape[2]
    sb_heads, sb_dh = cache_sb_k.shape[3], cache_sb_k.shape[4]
    band_heads, band_dh = cache_band_k.shape[3], cache_band_k.shape[4]
    band_rows = cache_band_k.shape[2]
    mla_heads = mla_w_uk.shape[2]
    lora = mla_w_uk.shape[1]
    mla_scale = (MLA_NOPE + MLA_ROPE) ** -0.5 * LOG2E

    tm_p, tm_s = 512, bs * dec
    tm_ffn = 512
    blk = 256
    mla_blk = 512
    fc = 256

    xp = x_prompt.reshape(bp * seq, d)
    xs = x_sample.reshape(bs * dec, d)
    outs = {name: [] for name in ("sb_kp", "sb_vp", "sb_ks", "sb_vs", "band_kp", "band_vp", "band_ks", "band_vs",
                                  "mla_cp", "mla_rp", "mla_cs", "mla_rs")}

    for layer in range(depth):
        kind, j = layer % 3, layer // 3
        g_mix = norm_mix[layer]
        if kind == 0:
            w_bf = sb_w_qkv[j].astype(BF16)
            scale = sb_dh ** -0.5 * LOG2E
            q, k, v, kb, vb = _qkv_proj(xp, g_mix, w_bf, scale, tm_p)
            ap = _sb_attention(q.reshape(bp, seq, d), kb.reshape(bp, seq, d), vb.reshape(bp, seq, d),
                               sb_heads, blk, blk, 0)
            outs["sb_kp"].append(k.reshape(bp, seq, sb_heads, sb_dh))
            outs["sb_vp"].append(v.reshape(bp, seq, sb_heads, sb_dh))
            q, k, v, kb, vb = _qkv_proj(xs, g_mix, w_bf, scale, tm_s)
            t_k = -(-(past + dec) // blk) * blk
            k_all, v_all = _sb_keys(cache_sb_k, cache_sb_v, j, kb.reshape(bs, dec, d), vb.reshape(bs, dec, d), t_k)
            a_s = _sb_attention(q.reshape(bs, dec, d), k_all, v_all, sb_heads, dec, blk, past)
            outs["sb_ks"].append(k.reshape(bs, dec, sb_heads, sb_dh))
            outs["sb_vs"].append(v.reshape(bs, dec, sb_heads, sb_dh))
            w_o = sb_w_o[j]
        elif kind == 1:
            w_bf = band_w_qkv[j].astype(BF16)
            scale = band_dh ** -0.5 * LOG2E
            win = BAND_CHUNKS * CHUNK
            q, k, v, kb, vb = _qkv_proj(xp, g_mix, w_bf, scale, tm_p, ones_v=True)
            tile = _band_tile(band_rel_bias[j], win, blk, 0, 3 * blk, 3 * blk) * LOG2E
            ap = _band_attention(q.reshape(bp, seq, d), kb.reshape(bp, seq, d), vb.reshape(bp, seq, 2 * d), tile,
                                 band_heads, blk, blk, True)
            keep = min(win, seq)
            outs["band_kp"].append(k.reshape(bp, seq, d)[:, seq - keep:].reshape(bp, keep, band_heads, band_dh))
            outs["band_vp"].append(v.reshape(bp, seq, d)[:, seq - keep:].reshape(bp, keep, band_heads, band_dh))
            q, k, v, kb, vb = _qkv_proj(xs, g_mix, w_bf, scale, tm_s, ones_v=True)
            n_keys = band_rows + dec
            k_all = _pad_rows(jnp.concatenate([cache_band_k[j].reshape(bs, band_rows, d).astype(BF16),
                                               kb.reshape(bs, dec, d)], axis=1), 3 * blk)
            cache_v = cache_band_v[j].reshape(bs, band_rows, d // 128, 128).astype(BF16)
            cache_v = jnp.concatenate([cache_v, jnp.ones_like(cache_v)], axis=-1).reshape(bs, band_rows, 2 * d)
            v_all = _pad_rows(jnp.concatenate([cache_v, vb.reshape(bs, dec, 2 * d)], axis=1), 3 * blk)
            tile = _band_tile(band_rel_bias[j], past, dec, past - band_rows, 3 * blk, n_keys) * LOG2E
            a_s = _band_attention(q.reshape(bs, dec, d), k_all, v_all, tile, band_heads, dec, blk, False)
            outs["band_ks"].append(k.reshape(bs, dec, band_heads, band_dh))
            outs["band_vs"].append(v.reshape(bs, dec, band_heads, band_dh))
            w_o = band_w_o[j]
        else:
            wts = _mla_weights(mla_w_dq[j], mla_q_norm[j], mla_w_uq[j], mla_w_dkv[j], mla_kv_norm[j],
                               mla_w_uk[j], mla_w_uv[j], mla_heads)
            cos_p, sin_p = _rope_tables(jnp.arange(seq))
            qc, kc, vv, ckv, kr = _mla_proj(xp, g_mix, wts, cos_p, sin_p, mla_heads, mla_scale, tm_p)
            ap = _mla_attention(qc.reshape(bp, seq, -1), kc.reshape(bp, seq, -1), vv.reshape(bp, seq, -1),
                                mla_heads, mla_blk, mla_blk, 0, seq)
            outs["mla_cp"].append(ckv.reshape(bp, seq, lora))
            outs["mla_rp"].append(kr.reshape(bp, seq, MLA_ROPE))
            cos_s, sin_s = _rope_tables(jnp.tile(past + jnp.arange(dec), bs))
            qc, kc, vv, ckv, kr = _mla_proj(xs, g_mix, wts, cos_s, sin_s, mla_heads, mla_scale, tm_s)
            a_s = _mla_sample_attention(qc.reshape(bs, dec, -1), cache_mla_ckv, cache_mla_krope, j,
                                        ckv.reshape(bs, dec, lora), kr.reshape(bs, dec, MLA_ROPE),
                                        wts[6], wts[7], mla_heads)
            outs["mla_cs"].append(ckv.reshape(bs, dec, lora))
            outs["mla_rs"].append(kr.reshape(bs, dec, MLA_ROPE))
            w_o = mla_w_o[j]

        wgu, wd = w_gate_up[layer].astype(BF16), w_down[layer].astype(BF16)
        final = layer == depth - 1
        wo_bf = w_o.astype(BF16)
        xp = _oproj_ffn(xp, ap.reshape(bp * seq, d), wo_bf, norm_ffn[layer], wgu, wd, norm_final, final, tm_ffn, fc)
        xs = _oproj_ffn(xs, a_s.reshape(bs * dec, d), wo_bf, norm_ffn[layer], wgu, wd, norm_final, final, tm_s, fc)

    st = lambda name: jnp.stack(outs[name])
    return (xp.reshape(bp, seq, d), xs.reshape(bs, dec, d),
            st("sb_kp"), st("sb_vp"), st("band_kp"), st("band_vp"), st("mla_cp"), st("mla_rp"),
            st("sb_ks"), st("sb_vs"), st("band_ks"), st("band_vs"), st("mla_cs"), st("mla_rs"))
```

```python
import functools

import jax
import jax.numpy as jnp
from jax import lax
from jax.experimental import pallas as pl
from jax.experimental.pallas import tpu as pltpu

CHUNK = 64
BAND_CHUNKS = 8
REL_MAX = 128
MLA_NOPE = 128
MLA_ROPE = 64
MLA_V = 128
ROPE_THETA = 10000.0
EPS = 1e-6

LOG2E = 1.4426950408889634
SB_LOG2_ZERO = -151.0

VMEM_LIMIT = 56 * 1024 * 1024
F32 = jnp.float32
BF16 = jnp.bfloat16


def _cparams(n_grid):
    return pltpu.CompilerParams(dimension_semantics=("arbitrary",) * n_grid, vmem_limit_bytes=VMEM_LIMIT)


def _rms(x, g):
    return x * lax.rsqrt(jnp.mean(x * x, axis=-1, keepdims=True) + EPS) * g


def _dot(a, b):
    return jnp.dot(a, b, preferred_element_type=F32)


def _dot_nt(a, b):
    return lax.dot_general(a, b, (((1,), (1,)), ((), ())), preferred_element_type=F32)


def _lane_tiles(x, op):
    tiles = [x[:, t:t + 128] for t in range(0, x.shape[1], 128)]
    while len(tiles) > 1:
        tiles = [op(tiles[t], tiles[t + 1]) for t in range(0, len(tiles) - 1, 2)] + tiles[len(tiles) & ~1:]
    return tiles[0]


def _const_spec(shape):
    nd = len(shape)
    return pl.BlockSpec(shape, lambda *_: (0,) * nd)


def _qkv_kernel(x_ref, g_ref, w_ref, *refs, d, q_scale, ones_v, heads_out, first_slot):
    q_ref, k_ref, v_ref, kb_ref, vb_ref = refs[-5:]
    h = _rms(x_ref[...], g_ref[...]).astype(BF16)
    q = _dot(h, w_ref[:, 0:d])
    q_ref[...] = (q * q_scale).astype(BF16)
    k = _dot(h, w_ref[:, d:2 * d])
    kb_ref[...] = k.astype(BF16)
    v = _dot(h, w_ref[:, 2 * d:3 * d])
    if heads_out is None:
        k_ref[...] = k
        v_ref[...] = v
    else:
        for o_ref, val in ((k_ref, k), (v_ref, v)):
            val = pltpu.einshape("r(hd)->rhd", val, h=heads_out)
            if first_slot:
                o_ref[0] = val
                o_ref[1:] = jnp.zeros((o_ref.shape[0] - 1,) + val.shape, F32)
            else:
                o_ref[...] = val
    if ones_v:
        ones = jnp.ones((k.shape[0], 128), BF16)
        for t in range(d // 128):
            vb_ref[:, 256 * t:256 * t + 128] = v[:, 128 * t:128 * (t + 1)].astype(BF16)
            vb_ref[:, 256 * t + 128:256 * (t + 1)] = ones
    else:
        vb_ref[...] = v.astype(BF16)


def _qkv_proj(x2d, g, w_bf, q_scale, tm, ones_v=False, stack=None):
    n, d = x2d.shape
    row = lambda i: (i, 0)
    wv = 2 * d if ones_v else d
    in_specs = [pl.BlockSpec((tm, d), row), _const_spec((1, d)), _const_spec((d, 3 * d))]
    args = [x2d, g.reshape(1, d), w_bf]
    heads_out, first_slot, aliases = None, False, {}
    kv_specs = [pl.BlockSpec((tm, d), row)] * 2
    kv_shapes = [jax.ShapeDtypeStruct((n, d), F32)] * 2
    if stack is not None:
        slot, n_slots, heads_out, k_buf, v_buf = stack
        first_slot = slot == 0
        kv_shapes = [jax.ShapeDtypeStruct((n_slots, n, heads_out, d // heads_out), F32)] * 2
        if first_slot:
            kv_specs = [pl.BlockSpec((n_slots, tm, heads_out, d // heads_out), lambda i: (0, i, 0, 0))] * 2
        else:
            kv_specs = [pl.BlockSpec((None, tm, heads_out, d // heads_out), lambda i: (slot, i, 0, 0))] * 2
            in_specs += [pl.BlockSpec(memory_space=pl.ANY)] * 2
            args += [k_buf, v_buf]
            aliases = {3: 1, 4: 2}
    return pl.pallas_call(
        functools.partial(_qkv_kernel, d=d, q_scale=q_scale, ones_v=ones_v, heads_out=heads_out,
                          first_slot=first_slot),
        grid=(n // tm,),
        in_specs=in_specs,
        out_specs=[pl.BlockSpec((tm, d), row)] + kv_specs + [pl.BlockSpec((tm, d), row), pl.BlockSpec((tm, wv), row)],
        out_shape=[jax.ShapeDtypeStruct((n, d), BF16)] + kv_shapes
        + [jax.ShapeDtypeStruct((n, d), BF16), jax.ShapeDtypeStruct((n, wv), BF16)],
        input_output_aliases=aliases,
        compiler_params=_cparams(1),
        name="qkv_proj",
    )(*args)


def _ffn_kernel(x_ref, a_ref, wo_ref, g_ref, wgu_ref, wd_ref, gf_ref, o_ref, acc_ref, h_ref, *, n_chunks, fc, final):
    x1 = x_ref[...] + _dot(a_ref[...], wo_ref[...])
    h_ref[...] = _rms(x1, g_ref[...]).astype(BF16)
    acc_ref[...] = x1

    def body(c, carry):
        lo = pl.multiple_of(c * fc, fc)
        gate = _dot(h_ref[...], wgu_ref[:, pl.ds(lo, fc)])
        up = _dot(h_ref[...], wgu_ref[:, pl.ds(pl.multiple_of(n_chunks * fc + lo, fc), fc)])
        act = (gate / (1.0 + jnp.exp(-gate)) * up).astype(BF16)
        acc_ref[...] += _dot(act, wd_ref[pl.ds(lo, fc), :])
        return carry

    lax.fori_loop(0, n_chunks, body, 0, unroll=True)
    out = acc_ref[...]
    if final:
        out = _rms(out, gf_ref[...])
    o_ref[...] = out


def _oproj_ffn(x2d, a2d, wo_bf, g, wgu_c, wd_c, g_final, final, tm, fc):
    n, d = x2d.shape
    n_chunks = wd_c.shape[0] // fc
    row = lambda i: (i, 0)
    return pl.pallas_call(
        functools.partial(_ffn_kernel, n_chunks=n_chunks, fc=fc, final=final),
        grid=(n // tm,),
        in_specs=[pl.BlockSpec((tm, d), row), pl.BlockSpec((tm, d), row), _const_spec((d, d)), _const_spec((1, d)),
                  _const_spec(wgu_c.shape), _const_spec(wd_c.shape), _const_spec((1, d))],
        out_specs=pl.BlockSpec((tm, d), row),
        out_shape=jax.ShapeDtypeStruct((n, d), F32),
        scratch_shapes=[pltpu.VMEM((tm, d), F32), pltpu.VMEM((tm, d), BF16)],
        compiler_params=_cparams(1),
        name="oproj_ffn",
    )(x2d, a2d, wo_bf, g.reshape(1, d), wgu_c, wd_c, g_final.reshape(1, d))


def _sb_kernel(q_ref, k_ref, v_ref, tri_ref, o_ref, a_ref, acc_ref, *, tq, tk, dh, n_heads, q_off):
    i = pl.program_id(2)
    q_start = q_off + i * tq
    q_pos = q_start + lax.broadcasted_iota(jnp.int32, (tq, 1), 0)
    col = lax.broadcasted_iota(jnp.int32, (1, tk), 1)
    j_last = (q_start + tq - 2) // tk
    a_ref[...] = jnp.zeros(a_ref.shape, F32)
    acc_ref[...] = jnp.zeros(acc_ref.shape, F32)

    def cond(state):
        j, a_max = state
        return jnp.logical_and(j >= 0, a_max > SB_LOG2_ZERO)

    def body(state):
        j, _ = state
        tri = tri_ref[...]
        lanes = [pl.ds(c * dh, dh) for c in range(n_heads)]
        starts = [pl.multiple_of(j * tk, tk), pl.multiple_of(jnp.maximum(j - 1, 0) * tk, tk)]
        masks = [(starts[0] + col) < q_pos, jnp.logical_and((starts[1] + col) < q_pos, j >= 1)]
        zs = [[jnp.where(mask, _dot_nt(q_ref[:, ln], k_ref[pl.ds(st, tk), ln]), -jnp.inf) for ln in lanes]
              for st, mask in zip(starts, masks)]
        softs = [[jnp.maximum(z, 0.0) + jnp.log(1.0 + jnp.exp2(-jnp.abs(z))) * LOG2E for z in zb] for zb in zs]
        rs = []
        for sb in softs:
            rb = []
            for soft in sb:
                rb.append(_dot(soft.astype(BF16), tri))
            rs.append(rb)
        a_max = jnp.float32(-jnp.inf)
        ws = [[], []]
        for c in range(n_heads):
            a = a_ref[c]
            for blk in range(2):
                a_wide = jnp.concatenate([a] * (tk // 128), axis=1)
                ws[blk].append(jnp.exp2(zs[blk][c] + rs[blk][c] + a_wide).astype(BF16))
                a = a - jnp.sum(_lane_tiles(softs[blk][c], jnp.add), axis=-1, keepdims=True)
            a_ref[c] = a
            a_max = jnp.maximum(a_max, jnp.max(a))
        for c in range(n_heads):
            acc_ref[:, lanes[c]] += (_dot(ws[0][c], v_ref[pl.ds(starts[0], tk), lanes[c]])
                                     + _dot(ws[1][c], v_ref[pl.ds(starts[1], tk), lanes[c]]))
        return j - 2, a_max

    lax.while_loop(cond, body, (j_last, jnp.float32(0.0)))
    o_ref[...] = acc_ref[...].astype(o_ref.dtype)


def _sb_keys_kernel(ck_ref, cv_ref, nk_ref, nv_ref, k_ref, v_ref, *, heads, past, dec):
    dh = ck_ref.shape[2]
    for c_ref, n_ref, o_ref in ((ck_ref, nk_ref, k_ref), (cv_ref, nv_ref, v_ref)):
        by_head = pltpu.einshape("rhd->hrd", c_ref[...])
        for hd in range(heads):
            o_ref[0:past, hd * dh:(hd + 1) * dh] = by_head[hd].astype(BF16)
        o_ref[past:past + dec, :] = n_ref[...]
        o_ref[past + dec:, :] = jnp.zeros((o_ref.shape[0] - past - dec, o_ref.shape[1]), BF16)


def _sb_keys(cache_k, cache_v, layer, new_k, new_v, t_k):
    _, b, past, heads, dh = cache_k.shape
    dec, d = new_k.shape[1], new_k.shape[2]
    cache_spec = pl.BlockSpec((None, None, past, heads, dh), lambda i: (layer, i, 0, 0, 0))
    new_spec = pl.BlockSpec((None, dec, d), lambda i: (i, 0, 0))
    out_spec = pl.BlockSpec((None, t_k, d), lambda i: (i, 0, 0))
    return pl.pallas_call(
        functools.partial(_sb_keys_kernel, heads=heads, past=past, dec=dec),
        grid=(b,),
        in_specs=[cache_spec, cache_spec, new_spec, new_spec],
        out_specs=[out_spec, out_spec],
        out_shape=[jax.ShapeDtypeStruct((b, t_k, d), BF16)] * 2,
        compiler_params=_cparams(1),
        name="sb_keys",
    )(cache_k, cache_v, new_k, new_v)


def _sb_attention(q, k, v, heads, tq, tk, q_off):
    b, t_q, d = q.shape
    t_k = k.shape[1]
    dh = d // heads
    n_heads = 2
    tri = -jnp.tril(jnp.ones((tk, tk), F32)).astype(BF16)
    return pl.pallas_call(
        functools.partial(_sb_kernel, tq=tq, tk=tk, dh=dh, n_heads=n_heads, q_off=q_off),
        grid=(b, heads // n_heads, t_q // tq),
        in_specs=[pl.BlockSpec((None, tq, n_heads * dh), lambda b_, h, i: (b_, i, h)),
                  pl.BlockSpec((None, t_k, n_heads * dh), lambda b_, h, i: (b_, 0, h)),
                  pl.BlockSpec((None, t_k, n_heads * dh), lambda b_, h, i: (b_, 0, h)),
                  _const_spec((tk, tk))],
        out_specs=pl.BlockSpec((None, tq, n_heads * dh), lambda b_, h, i: (b_, i, h)),
        out_shape=jax.ShapeDtypeStruct((b, t_q, d), BF16),
        scratch_shapes=[pltpu.VMEM((n_heads, tq, 128), F32), pltpu.VMEM((tq, n_heads * dh), F32)],
        compiler_params=_cparams(3),
        name="sb_attention",
    )(q, k, v, tri)


def _band_kernel(q_ref, k0_ref, k1_ref, k2_ref, v0_ref, v1_ref, v2_ref, tile_ref, o_ref, *, tk, pairs, prompt):
    tq = q_ref.shape[0]
    pair = 128
    half = pair // 2
    lane = lax.broadcasted_iota(jnp.int32, (1, pair), 1)
    col = lax.broadcasted_iota(jnp.int32, (1, 3 * tk), 1)

    def compute(first_col):
        scores = []
        for c in range(pairs):
            ln = pl.ds(c * pair, pair)
            q = q_ref[:, ln]
            kw = jnp.concatenate([k0_ref[:, ln], k1_ref[:, ln], k2_ref[:, ln]], axis=0)
            for hh in range(2):
                in_head = (lane >= hh * half) & (lane < (hh + 1) * half)
                s = _dot_nt(jnp.where(in_head, q, jnp.zeros_like(q)), kw) + tile_ref[2 * c + hh]
                if first_col is not None:
                    s = jnp.where(col >= first_col, s, -jnp.inf)
                scores.append(s)
        weights = []
        for s in scores:
            m = jnp.broadcast_to(jnp.max(_lane_tiles(s, jnp.maximum), axis=-1, keepdims=True), (tq, 128))
            weights.append(jnp.exp2(s - jnp.concatenate([m] * (3 * tk // 128), axis=1)).astype(BF16))
        for c in range(pairs):
            lv = pl.ds(c * 2 * pair, 2 * pair)
            vw = jnp.concatenate([v0_ref[:, lv], v1_ref[:, lv], v2_ref[:, lv]], axis=0)
            outs = []
            for hh in range(2):
                r = _dot(weights[2 * c + hh], vw)
                outs.append(r[:, :pair] / r[:, pair:])
            o_ref[:, pl.ds(c * pair, pair)] = jnp.where(lane < half, outs[0], outs[1]).astype(o_ref.dtype)

    if prompt:
        i = pl.program_id(2)

        @pl.when(i < 2)
        def _():
            compute((2 - i) * tk)

        @pl.when(i >= 2)
        def _():
            compute(None)
    else:
        compute(None)


def _band_tile(rel_bias, q0, n_q, k0, n_k, n_valid):
    heads = rel_bias.shape[0]
    n = n_q + n_k - 1
    dist = (q0 - k0) + (n_q - 1) - jnp.arange(n)
    g = rel_bias[:, jnp.clip(dist, -REL_MAX, REL_MAX) + REL_MAX].astype(F32)
    flat = jnp.tile(jnp.pad(g, ((0, 0), (0, 1))), (1, n_q))[:, :n_q * n]
    bias = flat.reshape(heads, n_q, n)[:, :, n_q - 1:]
    q_chunk = ((q0 + jnp.arange(n_q)) // CHUNK)[:, None]
    k_chunk = ((k0 + jnp.arange(n_k)) // CHUNK)[None, :]
    mask = (jnp.arange(n_k) < n_valid)[None, :] & (k_chunk <= q_chunk) & (k_chunk >= q_chunk - BAND_CHUNKS)
    return jnp.where(mask[None], bias, -jnp.inf)


def _band_attention(q, k, v, tile, heads, tq, tk, prompt):
    b, t_q, d = q.shape
    assert 2 * (d // heads) == 128
    pairs = 2
    w = 128 * pairs
    if prompt:
        kmap = lambda t: (lambda hp, b_, i: (b_, jnp.maximum(i - 2 + t, 0), hp))
    else:
        kmap = lambda t: (lambda hp, b_, i: (b_, t, hp))
    kspecs = [pl.BlockSpec((None, tk, w), kmap(t)) for t in range(3)]
    vspecs = [pl.BlockSpec((None, tk, 2 * w), kmap(t)) for t in range(3)]
    return pl.pallas_call(
        functools.partial(_band_kernel, tk=tk, pairs=pairs, prompt=prompt),
        grid=(heads // (2 * pairs), b, t_q // tq),
        in_specs=[pl.BlockSpec((None, tq, w), lambda hp, b_, i: (b_, i, hp))] + kspecs + vspecs
        + [pl.BlockSpec((2 * pairs, tq, 3 * tk), lambda hp, b_, i: (hp, 0, 0))],
        out_specs=pl.BlockSpec((None, tq, w), lambda hp, b_, i: (b_, i, hp)),
        out_shape=jax.ShapeDtypeStruct((b, t_q, d), BF16),
        compiler_params=_cparams(3),
        name="band_attention",
    )(q, k, k, k, v, v, v, tile)


def _mla_proj_kernel(x_ref, g_ref, wdq_ref, qn_ref, wqc_ref, wqs_ref, wdkv_ref, kvn_ref, wuk_ref, wuv_ref,
                     cos_ref, sin_ref, qc_ref, kc_ref, v_ref, ckv_ref, kr_ref, *, heads, scale):
    w = MLA_NOPE
    h = _rms(x_ref[...], g_ref[...]).astype(BF16)
    qn = _rms(_dot(h, wdq_ref[...]), qn_ref[...]).astype(BF16)
    a = _dot(qn, wqc_ref[...])
    sw = _dot(qn, wqs_ref[...])
    cos = cos_ref[...]
    sin = sin_ref[...]
    for hd in range(heads):
        qc_ref[:, 2 * w * hd:2 * w * hd + w] = (a[:, 2 * w * hd:2 * w * hd + w] * scale).astype(BF16)
        rope = a[:, 2 * w * hd + w:2 * w * (hd + 1)] * cos + sw[:, w * hd:w * (hd + 1)] * sin
        qc_ref[:, 2 * w * hd + w:2 * w * (hd + 1)] = (rope * scale).astype(BF16)
    kv = _dot(h, wdkv_ref[...])
    lora = kv.shape[1] - 2 * w
    ckv = _rms(kv[:, :lora], kvn_ref[...])
    ckv_ref[...] = ckv
    kr = kv[:, lora:lora + w] * cos + kv[:, lora + w:lora + 2 * w] * sin
    kr_ref[...] = kr[:, :MLA_ROPE]
    cb = ckv.astype(BF16)
    kn = _dot(cb, wuk_ref[...])
    vv = _dot(cb, wuv_ref[...])
    krb = kr.astype(BF16)
    ones = jnp.ones((krb.shape[0], w), BF16)
    for hd in range(heads):
        kc_ref[:, 2 * w * hd:2 * w * hd + w] = kn[:, w * hd:w * (hd + 1)].astype(BF16)
        kc_ref[:, 2 * w * hd + w:2 * w * (hd + 1)] = krb
        v_ref[:, 2 * w * hd:2 * w * hd + w] = vv[:, w * hd:w * (hd + 1)].astype(BF16)
        v_ref[:, 2 * w * hd + w:2 * w * (hd + 1)] = ones


def _mla_proj(x2d, g, wts, cos_t, sin_t, heads, scale, tm):
    n, d = x2d.shape
    wdq, qn, wqc, wqs, wdkv, kvn, wuk, wuv = wts
    lora = wuk.shape[0]
    n_pos = cos_t.shape[0] // tm
    row = lambda i: (i, 0)
    pos = lambda i: (i % n_pos, 0)
    return pl.pallas_call(
        functools.partial(_mla_proj_kernel, heads=heads, scale=scale),
        grid=(n // tm,),
        in_specs=[pl.BlockSpec((tm, d), row), _const_spec((1, d)), _const_spec(wdq.shape), _const_spec(qn.shape),
                  _const_spec(wqc.shape), _const_spec(wqs.shape), _const_spec(wdkv.shape), _const_spec(kvn.shape),
                  _const_spec(wuk.shape), _const_spec(wuv.shape),
                  pl.BlockSpec((tm, MLA_NOPE), pos), pl.BlockSpec((tm, MLA_NOPE), pos)],
        out_specs=[pl.BlockSpec((tm, heads * 2 * MLA_NOPE), row), pl.BlockSpec((tm, heads * 2 * MLA_NOPE), row),
                   pl.BlockSpec((tm, heads * 2 * MLA_V), row), pl.BlockSpec((tm, lora), row),
                   pl.BlockSpec((tm, MLA_ROPE), row)],
        out_shape=[jax.ShapeDtypeStruct((n, heads * 2 * MLA_NOPE), BF16),
                   jax.ShapeDtypeStruct((n, heads * 2 * MLA_NOPE), BF16),
                   jax.ShapeDtypeStruct((n, heads * 2 * MLA_V), BF16),
                   jax.ShapeDtypeStruct((n, lora), F32),
                   jax.ShapeDtypeStruct((n, MLA_ROPE), F32)],
        compiler_params=_cparams(1),
        name="mla_proj",
    )(x2d, g.reshape(1, d), wdq, qn, wqc, wqs, wdkv, kvn, wuk, wuv, cos_t, sin_t)


def _mla_attn_kernel(q_ref, kt_ref, v_ref, o_ref, m_ref, alpha_ref, p_ref, acc_ref, *, tq, tk, group, q_off, kv_len,
                     n_mask, mask_all):
    i = pl.program_id(2)
    q_start = q_off + i * tq
    q_chunk = (q_start + lax.broadcasted_iota(jnp.int32, (tq, 1), 0)) // CHUNK
    col = lax.broadcasted_iota(jnp.int32, (1, tk), 1)
    k_end = jnp.minimum(((q_start + tq - 1) // CHUNK + 1) * CHUNK, kv_len)
    n_blk = (k_end + tk - 1) // tk
    m_ref[...] = jnp.full(m_ref.shape, -jnp.inf, F32)
    acc_ref[...] = jnp.zeros(acc_ref.shape, F32)
    strip = min(tq, 64)

    n_rows = 2 if tq >= 256 else 1
    rh = tq // n_rows

    def process(blocks):
        starts = [pl.multiple_of(j * tk, tk) for j, _ in blocks]
        scores = [[_dot(q_ref[pl.ds(h * rh, rh), :], kt_ref[:, pl.ds(st, tk)]) for h in range(n_rows)]
                  for st in starts]
        for n, ((_, masked), st, s_halves) in enumerate(zip(blocks, starts, scores)):
            c = n % 2
            for r0 in range(0, tq, strip):
                rows = pl.ds(r0, strip)
                s = s_halves[r0 // rh][r0 % rh:r0 % rh + strip, :]
                if masked:
                    k_pos = st + col
                    s = jnp.where(((k_pos // CHUNK) <= q_chunk[r0:r0 + strip]) & (k_pos < kv_len), s, -jnp.inf)
                m_prev = m_ref[c, rows, :]
                m_new = jnp.maximum(m_prev, jnp.max(_lane_tiles(s, jnp.maximum), axis=-1, keepdims=True))
                p_ref[n, rows, :] = jnp.exp2(s - jnp.concatenate([m_new] * (tk // 128), axis=1)).astype(BF16)
                alpha_ref[n, rows, :] = jnp.exp2(m_prev - m_new)
                m_ref[c, rows, :] = m_new
        for n, st in enumerate(starts):
            c = n % 2
            for h in range(n_rows):
                rows = pl.ds(h * rh, rh)
                alpha = alpha_ref[n, rows, :]
                acc_ref[c, rows, :] = (jnp.concatenate([alpha, alpha], axis=1) * acc_ref[c, rows, :]
                                       + _dot(p_ref[n, rows, :], v_ref[pl.ds(st, tk), :]))

    def body(g, carry):
        process([(group * g + n, mask_all) for n in range(group)])
        return carry

    n_old = n_blk - n_mask
    n_groups = n_old // group
    lax.fori_loop(0, n_groups, body, 0)
    first = group * n_groups
    for rest in range(group):
        @pl.when(n_old - first == rest)
        def _(rest=rest):
            process([(first + n, mask_all) for n in range(rest)]
                    + [(first + rest + n, True) for n in range(n_mask)])

    m0, m1 = m_ref[0], m_ref[1]
    m = jnp.maximum(m0, m1)
    a0 = jnp.exp2(m0 - m)
    a1 = jnp.exp2(m1 - m)
    acc = jnp.concatenate([a0, a0], axis=1) * acc_ref[0] + jnp.concatenate([a1, a1], axis=1) * acc_ref[1]
    o_ref[...] = (acc[:, :MLA_V] / acc[:, MLA_V:]).astype(o_ref.dtype)


def _mla_attention(qc, kc, v1, heads, tq, tk, q_off, kv_len, group=4):
    b, t_q, _ = qc.shape
    t_k = kc.shape[1]
    w = 2 * MLA_NOPE
    mask_all = not (q_off == 0 and (tq % tk == 0 or tk % tq == 0) and t_q == t_k == kv_len)
    n_mask = 1 if mask_all else max(1, tq // tk)
    slots = group - 1 + n_mask
    kt = jnp.transpose(kc, (0, 2, 1))
    return pl.pallas_call(
        functools.partial(_mla_attn_kernel, tq=tq, tk=tk, group=group, q_off=q_off, kv_len=kv_len,
                          n_mask=n_mask, mask_all=mask_all),
        grid=(b, heads, t_q // tq),
        in_specs=[pl.BlockSpec((None, tq, w), lambda b_, h, i: (b_, i, h)),
                  pl.BlockSpec((None, w, t_k), lambda b_, h, i: (b_, h, 0)),
                  pl.BlockSpec((None, t_k, MLA_V + 128), lambda b_, h, i: (b_, 0, h))],
        out_specs=pl.BlockSpec((None, tq, MLA_V), lambda b_, h, i: (b_, i, h)),
        out_shape=jax.ShapeDtypeStruct((b, t_q, heads * MLA_V), BF16),
        scratch_shapes=[pltpu.VMEM((2, tq, 128), F32), pltpu.VMEM((slots, tq, 128), F32),
                        pltpu.VMEM((slots, tq, tk), BF16),
                        pltpu.VMEM((2, tq, MLA_V + 128), F32)],
        compiler_params=_cparams(3),
        name="mla_attention",
    )(qc, kt, v1)


def _mla_sample_kernel(q_ref, cc_ref, cr_ref, nc_ref, nr_ref, wuk_ref, wuv_ref, o_ref, c_all, r_all, *,
                       heads, past, dec):
    w = MLA_NOPE
    t_k = c_all.shape[0]
    c_all[0:past, :] = cc_ref[...].astype(BF16)
    c_all[past:past + dec, :] = nc_ref[...].astype(BF16)
    c_all[past + dec:, :] = jnp.zeros((t_k - past - dec, c_all.shape[1]), BF16)
    r_all[0:past, :] = cr_ref[...].astype(BF16)
    r_all[past:past + dec, :] = nr_ref[...].astype(BF16)
    r_all[past + dec:, :] = jnp.zeros((t_k - past - dec, r_all.shape[1]), BF16)
    q_lat = jnp.concatenate([_dot_nt(q_ref[:, 2 * w * hd:2 * w * hd + w],
                                     wuk_ref[:, w * hd:w * (hd + 1)]).astype(BF16) for hd in range(heads)], axis=0)
    q_rope = jnp.concatenate([q_ref[:, 2 * w * hd + w:2 * w * hd + w + MLA_ROPE] for hd in range(heads)], axis=0)
    k_pos = lax.broadcasted_iota(jnp.int32, (1, t_k), 1)
    q_chunk = (past + lax.broadcasted_iota(jnp.int32, (heads * dec, 1), 0) % dec) // CHUNK
    s = _dot_nt(q_lat, c_all[...]) + _dot_nt(q_rope, r_all[...])
    s = jnp.where(((k_pos // CHUNK) <= q_chunk) & (k_pos < past + dec), s, -jnp.inf)
    p = jnp.exp2(s - jnp.max(s, axis=-1, keepdims=True))
    o_lat = (_dot(p.astype(BF16), c_all[...]) / jnp.sum(p, axis=-1, keepdims=True)).astype(BF16)
    for hd in range(heads):
        o_ref[:, w * hd:w * (hd + 1)] = _dot(o_lat[hd * dec:(hd + 1) * dec, :],
                                             wuv_ref[:, w * hd:w * (hd + 1)]).astype(o_ref.dtype)


def _mla_sample_attention(qc, cache_ckv, cache_krope, layer, ckv_new, kr_new, wuk, wuv, heads):
    b, dec, _ = qc.shape
    past, lora = cache_ckv.shape[2], cache_ckv.shape[3]
    t_k = -(-(past + dec) // 128) * 128
    cache = lambda width: pl.BlockSpec((None, None, past, width), lambda i: (layer, i, 0, 0))
    new = lambda width: pl.BlockSpec((None, dec, width), lambda i: (i, 0, 0))
    return pl.pallas_call(
        functools.partial(_mla_sample_kernel, heads=heads, past=past, dec=dec),
        grid=(b,),
        in_specs=[new(qc.shape[2]), cache(lora), cache(MLA_ROPE), new(lora), new(MLA_ROPE),
                  _const_spec(wuk.shape), _const_spec(wuv.shape)],
        out_specs=new(heads * MLA_V),
        out_shape=jax.ShapeDtypeStruct((b, dec, heads * MLA_V), BF16),
        scratch_shapes=[pltpu.VMEM((t_k, lora), BF16), pltpu.VMEM((t_k, MLA_ROPE), BF16)],
        compiler_params=_cparams(1),
        name="mla_sample_attention",
    )(qc, cache_ckv, cache_krope, ckv_new, kr_new, wuk, wuv)


def _rope_tables(pos):
    half = MLA_ROPE // 2
    inv_freq = ROPE_THETA ** (-jnp.arange(half, dtype=F32) / half)
    ang = pos.astype(F32)[:, None] * inv_freq[None, :]
    zeros = jnp.zeros((pos.shape[0], MLA_NOPE - MLA_ROPE), F32)
    cos, sin = jnp.cos(ang), jnp.sin(ang)
    return jnp.concatenate([cos, cos, zeros], axis=1), jnp.concatenate([sin, sin, zeros], axis=1)


def _swap_halves(w):
    half = w.shape[-1] // 2
    return jnp.concatenate([-w[..., half:], w[..., :half]], axis=-1)


def _mla_weights(w_dq, q_norm, w_uq, w_dkv, kv_norm, w_uk, w_uv, heads):
    q_lora = w_dq.shape[1]
    lora = w_uk.shape[0]
    pad = MLA_NOPE - MLA_ROPE
    uq = w_uq.reshape(q_lora, heads, MLA_NOPE + MLA_ROPE)
    zq = jnp.zeros((q_lora, heads, pad), F32)
    wqc = jnp.concatenate([uq, zq], axis=-1).reshape(q_lora, heads * 2 * MLA_NOPE)
    wqs = jnp.concatenate([_swap_halves(uq[..., MLA_NOPE:]), zq], axis=-1).reshape(q_lora, heads * MLA_NOPE)
    zk = jnp.zeros((w_dkv.shape[0], pad), F32)
    rope = w_dkv[:, lora:]
    wdkv = jnp.concatenate([w_dkv[:, :lora], rope, zk, _swap_halves(rope), zk], axis=1)
    return (w_dq.astype(BF16), q_norm.reshape(1, q_lora), wqc.astype(BF16), wqs.astype(BF16), wdkv.astype(BF16),
            kv_norm.reshape(1, lora), w_uk.reshape(lora, heads * MLA_NOPE).astype(BF16),
            w_uv.reshape(lora, heads * MLA_V).astype(BF16))


def _pad_rows(x, rows):
    return jnp.pad(x, ((0, 0), (0, rows - x.shape[1]), (0, 0)))


def kernel(x_prompt, x_sample, cache_sb_k, cache_sb_v, cache_band_k, cache_band_v, cache_mla_ckv, cache_mla_krope, norm_mix, norm_ffn, norm_final, w_gate_up, w_down, sb_w_qkv, sb_w_o, band_w_qkv, band_rel_bias, band_w_o, mla_w_dq, mla_q_norm, mla_w_uq, mla_w_dkv, mla_kv_norm, mla_w_uk, mla_w_uv, mla_w_o):
    bp, seq, d = x_prompt.shape
    bs, dec, _ = x_sample.shape
    depth = norm_mix.shape[0]
    past = cache_sb_k.shape[2]
    sb_heads, sb_dh = cache_sb_k.shape[3], cache_sb_k.shape[4]
    band_heads, band_dh = cache_band_k.shape[3], cache_band_k.shape[4]
    band_rows = cache_band_k.shape[2]
    mla_heads = mla_w_uk.shape[2]
    lora = mla_w_uk.shape[1]
    mla_scale = (MLA_NOPE + MLA_ROPE) ** -0.5 * LOG2E

    tm_p, tm_s = 512, bs * dec
    tm_ffn = 512
    blk = 256
    mla_blk = 512
    fc = 256

    xp = x_prompt.reshape(bp * seq, d)
    xs = x_sample.reshape(bs * dec, d)
    outs = {name: [] for name in ("sb_ks", "sb_vs", "band_kp", "band_vp", "band_ks", "band_vs",
                                  "mla_cp", "mla_rp", "mla_cs", "mla_rs")}
    n_sb = sb_w_qkv.shape[0]
    sb_kp = sb_vp = None

    for layer in range(depth):
        kind, j = layer % 3, layer // 3
        g_mix = norm_mix[layer]
        if kind == 0:
            w_bf = sb_w_qkv[j].astype(BF16)
            scale = sb_dh ** -0.5 * LOG2E
            q, sb_kp, sb_vp, kb, vb = _qkv_proj(xp, g_mix, w_bf, scale, tm_p,
                                                stack=(j, n_sb, sb_heads, sb_kp, sb_vp))
            ap = _sb_attention(q.reshape(bp, seq, d), kb.reshape(bp, seq, d), vb.reshape(bp, seq, d),
                               sb_heads, blk, blk, 0)
            q, k, v, kb, vb = _qkv_proj(xs, g_mix, w_bf, scale, tm_s)
            t_k = -(-(past + dec) // blk) * blk
            k_all, v_all = _sb_keys(cache_sb_k, cache_sb_v, j, kb.reshape(bs, dec, d), vb.reshape(bs, dec, d), t_k)
            a_s = _sb_attention(q.reshape(bs, dec, d), k_all, v_all, sb_heads, dec, blk, past)
            outs["sb_ks"].append(k.reshape(bs, dec, sb_heads, sb_dh))
            outs["sb_vs"].append(v.reshape(bs, dec, sb_heads, sb_dh))
            w_o = sb_w_o[j]
        elif kind == 1:
            w_bf = band_w_qkv[j].astype(BF16)
            scale = band_dh ** -0.5 * LOG2E
            win = BAND_CHUNKS * CHUNK
            q, k, v, kb, vb = _qkv_proj(xp, g_mix, w_bf, scale, tm_p, ones_v=True)
            tile = _band_tile(band_rel_bias[j], win, blk, 0, 3 * blk, 3 * blk) * LOG2E
            ap = _band_attention(q.reshape(bp, seq, d), kb.reshape(bp, seq, d), vb.reshape(bp, seq, 2 * d), tile,
                                 band_heads, blk, blk, True)
            keep = min(win, seq)
            outs["band_kp"].append(k.reshape(bp, seq, d)[:, seq - keep:].reshape(bp, keep, band_heads, band_dh))
            outs["band_vp"].append(v.reshape(bp, seq, d)[:, seq - keep:].reshape(bp, keep, band_heads, band_dh))
            q, k, v, kb, vb = _qkv_proj(xs, g_mix, w_bf, scale, tm_s, ones_v=True)
            n_keys = band_rows + dec
            k_all = _pad_rows(jnp.concatenate([cache_band_k[j].reshape(bs, band_rows, d).astype(BF16),
                                               kb.reshape(bs, dec, d)], axis=1), 3 * blk)
            cache_v = cache_band_v[j].reshape(bs, band_rows, d // 128, 128).astype(BF16)
            cache_v = jnp.concatenate([cache_v, jnp.ones_like(cache_v)], axis=-1).reshape(bs, band_rows, 2 * d)
            v_all = _pad_rows(jnp.concatenate([cache_v, vb.reshape(bs, dec, 2 * d)], axis=1), 3 * blk)
            tile = _band_tile(band_rel_bias[j], past, dec, past - band_rows, 3 * blk, n_keys) * LOG2E
            a_s = _band_attention(q.reshape(bs, dec, d), k_all, v_all, tile, band_heads, dec, blk, False)
            outs["band_ks"].append(k.reshape(bs, dec, band_heads, band_dh))
            outs["band_vs"].append(v.reshape(bs, dec, band_heads, band_dh))
            w_o = band_w_o[j]
        else:
            wts = _mla_weights(mla_w_dq[j], mla_q_norm[j], mla_w_uq[j], mla_w_dkv[j], mla_kv_norm[j],
                               mla_w_uk[j], mla_w_uv[j], mla_heads)
            cos_p, sin_p = _rope_tables(jnp.arange(seq))
            qc, kc, vv, ckv, kr = _mla_proj(xp, g_mix, wts, cos_p, sin_p, mla_heads, mla_scale, tm_p)
            ap = _mla_attention(qc.reshape(bp, seq, -1), kc.reshape(bp, seq, -1), vv.reshape(bp, seq, -1),
                                mla_heads, mla_blk, mla_blk, 0, seq)
            outs["mla_cp"].append(ckv.reshape(bp, seq, lora))
            outs["mla_rp"].append(kr.reshape(bp, seq, MLA_ROPE))
            cos_s, sin_s = _rope_tables(jnp.tile(past + jnp.arange(dec), bs))
            qc, kc, vv, ckv, kr = _mla_proj(xs, g_mix, wts, cos_s, sin_s, mla_heads, mla_scale, tm_s)
            a_s = _mla_sample_attention(qc.reshape(bs, dec, -1), cache_mla_ckv, cache_mla_krope, j,
                                        ckv.reshape(bs, dec, lora), kr.reshape(bs, dec, MLA_ROPE),
                                        wts[6], wts[7], mla_heads)
            outs["mla_cs"].append(ckv.reshape(bs, dec, lora))
            outs["mla_rs"].append(kr.reshape(bs, dec, MLA_ROPE))
            w_o = mla_w_o[j]

        wgu, wd = w_gate_up[layer].astype(BF16), w_down[layer].astype(BF16)
        final = layer == depth - 1
        wo_bf = w_o.astype(BF16)
        xp = _oproj_ffn(xp, ap.reshape(bp * seq, d), wo_bf, norm_ffn[layer], wgu, wd, norm_final, final, tm_ffn, fc)
        xs = _oproj_ffn(xs, a_s.reshape(bs * dec, d), wo_bf, norm_ffn[layer], wgu, wd, norm_final, final, tm_s, fc)

    st = lambda name: jnp.stack(outs[name])
    return (xp.reshape(bp, seq, d), xs.reshape(bs, dec, d),
            sb_kp.reshape(n_sb, bp, seq, sb_heads, sb_dh), sb_vp.reshape(n_sb, bp, seq, sb_heads, sb_dh),
            st("band_kp"), st("band_vp"), st("mla_cp"), st("mla_rp"),
            st("sb_ks"), st("sb_vs"), st("band_ks"), st("band_vs"), st("mla_cs"), st("mla_rs"))
```

```python
import functools

import jax
import jax.numpy as jnp
from jax import lax
from jax.experimental import pallas as pl
from jax.experimental.pallas import tpu as pltpu

CHUNK = 64
BAND_CHUNKS = 8
REL_MAX = 128
MLA_NOPE = 128
MLA_ROPE = 64
MLA_V = 128
ROPE_THETA = 10000.0
EPS = 1e-6

LOG2E = 1.4426950408889634
SB_LOG2_ZERO = -151.0

VMEM_LIMIT = 56 * 1024 * 1024
F32 = jnp.float32
BF16 = jnp.bfloat16


def _cparams(n_grid):
    return pltpu.CompilerParams(dimension_semantics=("arbitrary",) * n_grid, vmem_limit_bytes=VMEM_LIMIT)


def _rms(x, g):
    return x * lax.rsqrt(jnp.mean(x * x, axis=-1, keepdims=True) + EPS) * g


def _dot(a, b):
    return jnp.dot(a, b, preferred_element_type=F32)


def _dot_nt(a, b):
    return lax.dot_general(a, b, (((1,), (1,)), ((), ())), preferred_element_type=F32)


def _lane_tiles(x, op):
    tiles = [x[:, t:t + 128] for t in range(0, x.shape[1], 128)]
    while len(tiles) > 1:
        tiles = [op(tiles[t], tiles[t + 1]) for t in range(0, len(tiles) - 1, 2)] + tiles[len(tiles) & ~1:]
    return tiles[0]


def _const_spec(shape):
    nd = len(shape)
    return pl.BlockSpec(shape, lambda *_: (0,) * nd)


def _qkv_kernel(x_ref, g_ref, w_ref, *refs, d, q_scale, ones_v, heads_out, first_slot):
    q_ref, k_ref, v_ref, kb_ref, vb_ref = refs[-5:]
    h = _rms(x_ref[...], g_ref[...]).astype(BF16)
    q = _dot(h, w_ref[:, 0:d])
    q_ref[...] = (q * q_scale).astype(BF16)
    k = _dot(h, w_ref[:, d:2 * d])
    kb_ref[...] = k.astype(BF16)
    v = _dot(h, w_ref[:, 2 * d:3 * d])
    if heads_out is None:
        k_ref[...] = k
        v_ref[...] = v
    else:
        for o_ref, val in ((k_ref, k), (v_ref, v)):
            val = pltpu.einshape("r(hd)->rhd", val, h=heads_out)
            if first_slot:
                o_ref[0] = val
                o_ref[1:] = jnp.zeros((o_ref.shape[0] - 1,) + val.shape, F32)
            else:
                o_ref[...] = val
    if ones_v:
        ones = jnp.ones((k.shape[0], 128), BF16)
        for t in range(d // 128):
            vb_ref[:, 256 * t:256 * t + 128] = v[:, 128 * t:128 * (t + 1)].astype(BF16)
            vb_ref[:, 256 * t + 128:256 * (t + 1)] = ones
    else:
        vb_ref[...] = v.astype(BF16)


def _qkv_proj(x2d, g, w_bf, q_scale, tm, ones_v=False, stack=None):
    n, d = x2d.shape
    row = lambda i: (i, 0)
    wv = 2 * d if ones_v else d
    in_specs = [pl.BlockSpec((tm, d), row), _const_spec((1, d)), _const_spec((d, 3 * d))]
    args = [x2d, g.reshape(1, d), w_bf]
    heads_out, first_slot, aliases = None, False, {}
    kv_specs = [pl.BlockSpec((tm, d), row)] * 2
    kv_shapes = [jax.ShapeDtypeStruct((n, d), F32)] * 2
    if stack is not None:
        slot, n_slots, heads_out, k_buf, v_buf = stack
        first_slot = slot == 0
        kv_shapes = [jax.ShapeDtypeStruct((n_slots, n, heads_out, d // heads_out), F32)] * 2
        if first_slot:
            kv_specs = [pl.BlockSpec((n_slots, tm, heads_out, d // heads_out), lambda i: (0, i, 0, 0))] * 2
        else:
            kv_specs = [pl.BlockSpec((None, tm, heads_out, d // heads_out), lambda i: (slot, i, 0, 0))] * 2
            in_specs += [pl.BlockSpec(memory_space=pl.ANY)] * 2
            args += [k_buf, v_buf]
            aliases = {3: 1, 4: 2}
    return pl.pallas_call(
        functools.partial(_qkv_kernel, d=d, q_scale=q_scale, ones_v=ones_v, heads_out=heads_out,
                          first_slot=first_slot),
        grid=(n // tm,),
        in_specs=in_specs,
        out_specs=[pl.BlockSpec((tm, d), row)] + kv_specs + [pl.BlockSpec((tm, d), row), pl.BlockSpec((tm, wv), row)],
        out_shape=[jax.ShapeDtypeStruct((n, d), BF16)] + kv_shapes
        + [jax.ShapeDtypeStruct((n, d), BF16), jax.ShapeDtypeStruct((n, wv), BF16)],
        input_output_aliases=aliases,
        compiler_params=_cparams(1),
        name="qkv_proj",
    )(*args)


def _ffn_kernel(x_ref, a_ref, wo_ref, g_ref, wgu_ref, wd_ref, gf_ref, o_ref, acc_ref, h_ref, *, n_chunks, fc, final):
    x1 = x_ref[...] + _dot(a_ref[...], wo_ref[...])
    h_ref[...] = _rms(x1, g_ref[...]).astype(BF16)
    acc_ref[...] = x1

    def body(c, carry):
        lo = pl.multiple_of(c * fc, fc)
        gate = _dot(h_ref[...], wgu_ref[:, pl.ds(lo, fc)])
        up = _dot(h_ref[...], wgu_ref[:, pl.ds(pl.multiple_of(n_chunks * fc + lo, fc), fc)])
        act = (gate / (1.0 + jnp.exp(-gate)) * up).astype(BF16)
        acc_ref[...] += _dot(act, wd_ref[pl.ds(lo, fc), :])
        return carry

    lax.fori_loop(0, n_chunks, body, 0, unroll=True)
    out = acc_ref[...]
    if final:
        out = _rms(out, gf_ref[...])
    o_ref[...] = out


def _oproj_ffn(x2d, a2d, wo_bf, g, wgu_c, wd_c, g_final, final, tm, fc):
    n, d = x2d.shape
    n_chunks = wd_c.shape[0] // fc
    row = lambda i: (i, 0)
    return pl.pallas_call(
        functools.partial(_ffn_kernel, n_chunks=n_chunks, fc=fc, final=final),
        grid=(n // tm,),
        in_specs=[pl.BlockSpec((tm, d), row), pl.BlockSpec((tm, d), row), _const_spec((d, d)), _const_spec((1, d)),
                  _const_spec(wgu_c.shape), _const_spec(wd_c.shape), _const_spec((1, d))],
        out_specs=pl.BlockSpec((tm, d), row),
        out_shape=jax.ShapeDtypeStruct((n, d), F32),
        scratch_shapes=[pltpu.VMEM((tm, d), F32), pltpu.VMEM((tm, d), BF16)],
        compiler_params=_cparams(1),
        name="oproj_ffn",
    )(x2d, a2d, wo_bf, g.reshape(1, d), wgu_c, wd_c, g_final.reshape(1, d))


def _sb_kernel(q_ref, k_ref, v_ref, tri_ref, o_ref, a_ref, acc_ref, *, tq, tk, dh, n_heads, q_off):
    i = pl.program_id(2)
    q_start = q_off + i * tq
    q_pos = q_start + lax.broadcasted_iota(jnp.int32, (tq, 1), 0)
    col = lax.broadcasted_iota(jnp.int32, (1, tk), 1)
    j_last = (q_start + tq - 2) // tk
    a_ref[...] = jnp.zeros(a_ref.shape, F32)
    acc_ref[...] = jnp.zeros(acc_ref.shape, F32)

    def cond(state):
        j, a_max = state
        return jnp.logical_and(j >= 0, a_max > SB_LOG2_ZERO)

    def body(state):
        j, _ = state
        tri = tri_ref[...]
        lanes = [pl.ds(c * dh, dh) for c in range(n_heads)]
        starts = [pl.multiple_of(j * tk, tk), pl.multiple_of(jnp.maximum(j - 1, 0) * tk, tk)]
        masks = [(starts[0] + col) < q_pos, jnp.logical_and((starts[1] + col) < q_pos, j >= 1)]
        zs = [[jnp.where(mask, _dot_nt(q_ref[:, ln], k_ref[pl.ds(st, tk), ln]), -jnp.inf) for ln in lanes]
              for st, mask in zip(starts, masks)]
        softs = [[jnp.maximum(z, 0.0) + jnp.log(1.0 + jnp.exp2(-jnp.abs(z))) * LOG2E for z in zb] for zb in zs]
        rs = []
        for sb in softs:
            rb = []
            for soft in sb:
                rb.append(_dot(soft.astype(BF16), tri))
            rs.append(rb)
        a_max = jnp.float32(-jnp.inf)
        ws = [[], []]
        for c in range(n_heads):
            a = a_ref[c]
            for blk in range(2):
                a_wide = jnp.concatenate([a] * (tk // 128), axis=1)
                ws[blk].append(jnp.exp2(zs[blk][c] + rs[blk][c] + a_wide).astype(BF16))
                a = a - jnp.sum(_lane_tiles(softs[blk][c], jnp.add), axis=-1, keepdims=True)
            a_ref[c] = a
            a_max = jnp.maximum(a_max, jnp.max(a))
        for c in range(n_heads):
            acc_ref[:, lanes[c]] += (_dot(ws[0][c], v_ref[pl.ds(starts[0], tk), lanes[c]])
                                     + _dot(ws[1][c], v_ref[pl.ds(starts[1], tk), lanes[c]]))
        return j - 2, a_max

    lax.while_loop(cond, body, (j_last, jnp.float32(0.0)))
    o_ref[...] = acc_ref[...].astype(o_ref.dtype)


def _sb_keys_kernel(ck_ref, cv_ref, nk_ref, nv_ref, k_ref, v_ref, *, heads, past, dec):
    dh = ck_ref.shape[2]
    for c_ref, n_ref, o_ref in ((ck_ref, nk_ref, k_ref), (cv_ref, nv_ref, v_ref)):
        by_head = pltpu.einshape("rhd->hrd", c_ref[...])
        for hd in range(heads):
            o_ref[0:past, hd * dh:(hd + 1) * dh] = by_head[hd].astype(BF16)
        o_ref[past:past + dec, :] = n_ref[...]
        o_ref[past + dec:, :] = jnp.zeros((o_ref.shape[0] - past - dec, o_ref.shape[1]), BF16)


def _sb_keys(cache_k, cache_v, layer, new_k, new_v, t_k):
    _, b, past, heads, dh = cache_k.shape
    dec, d = new_k.shape[1], new_k.shape[2]
    cache_spec = pl.BlockSpec((None, None, past, heads, dh), lambda i: (layer, i, 0, 0, 0))
    new_spec = pl.BlockSpec((None, dec, d), lambda i: (i, 0, 0))
    out_spec = pl.BlockSpec((None, t_k, d), lambda i: (i, 0, 0))
    return pl.pallas_call(
        functools.partial(_sb_keys_kernel, heads=heads, past=past, dec=dec),
        grid=(b,),
        in_specs=[cache_spec, cache_spec, new_spec, new_spec],
        out_specs=[out_spec, out_spec],
        out_shape=[jax.ShapeDtypeStruct((b, t_k, d), BF16)] * 2,
        compiler_params=_cparams(1),
        name="sb_keys",
    )(cache_k, cache_v, new_k, new_v)


def _sb_attention(q, k, v, heads, tq, tk, q_off):
    b, t_q, d = q.shape
    t_k = k.shape[1]
    dh = d // heads
    n_heads = 2
    tri = -jnp.tril(jnp.ones((tk, tk), F32)).astype(BF16)
    return pl.pallas_call(
        functools.partial(_sb_kernel, tq=tq, tk=tk, dh=dh, n_heads=n_heads, q_off=q_off),
        grid=(b, heads // n_heads, t_q // tq),
        in_specs=[pl.BlockSpec((None, tq, n_heads * dh), lambda b_, h, i: (b_, i, h)),
                  pl.BlockSpec((None, t_k, n_heads * dh), lambda b_, h, i: (b_, 0, h)),
                  pl.BlockSpec((None, t_k, n_heads * dh), lambda b_, h, i: (b_, 0, h)),
                  _const_spec((tk, tk))],
        out_specs=pl.BlockSpec((None, tq, n_heads * dh), lambda b_, h, i: (b_, i, h)),
        out_shape=jax.ShapeDtypeStruct((b, t_q, d), BF16),
        scratch_shapes=[pltpu.VMEM((n_heads, tq, 128), F32), pltpu.VMEM((tq, n_heads * dh), F32)],
        compiler_params=_cparams(3),
        name="sb_attention",
    )(q, k, v, tri)


def _band_kernel(q_ref, k0_ref, k1_ref, k2_ref, v0_ref, v1_ref, v2_ref, tile_ref, o_ref, *, tk, pairs, prompt):
    tq = q_ref.shape[0]
    pair = 128
    half = pair // 2
    lane = lax.broadcasted_iota(jnp.int32, (1, pair), 1)
    col = lax.broadcasted_iota(jnp.int32, (1, 3 * tk), 1)

    def compute(first_col):
        scores = []
        for c in range(pairs):
            ln = pl.ds(c * pair, pair)
            q = q_ref[:, ln]
            kw = jnp.concatenate([k0_ref[:, ln], k1_ref[:, ln], k2_ref[:, ln]], axis=0)
            for hh in range(2):
                in_head = (lane >= hh * half) & (lane < (hh + 1) * half)
                s = _dot_nt(jnp.where(in_head, q, jnp.zeros_like(q)), kw) + tile_ref[2 * c + hh]
                if first_col is not None:
                    s = jnp.where(col >= first_col, s, -jnp.inf)
                scores.append(s)
        weights = []
        for s in scores:
            m = jnp.broadcast_to(jnp.max(_lane_tiles(s, jnp.maximum), axis=-1, keepdims=True), (tq, 128))
            weights.append(jnp.exp2(s - jnp.concatenate([m] * (3 * tk // 128), axis=1)).astype(BF16))
        for c in range(pairs):
            lv = pl.ds(c * 2 * pair, 2 * pair)
            vw = jnp.concatenate([v0_ref[:, lv], v1_ref[:, lv], v2_ref[:, lv]], axis=0)
            outs = []
            for hh in range(2):
                r = _dot(weights[2 * c + hh], vw)
                outs.append(r[:, :pair] / r[:, pair:])
            o_ref[:, pl.ds(c * pair, pair)] = jnp.where(lane < half, outs[0], outs[1]).astype(o_ref.dtype)

    if prompt:
        i = pl.program_id(2)

        @pl.when(i < 2)
        def _():
            compute((2 - i) * tk)

        @pl.when(i >= 2)
        def _():
            compute(None)
    else:
        compute(None)


def _band_tile(rel_bias, q0, n_q, k0, n_k, n_valid):
    heads = rel_bias.shape[0]
    n = n_q + n_k - 1
    dist = (q0 - k0) + (n_q - 1) - jnp.arange(n)
    g = rel_bias[:, jnp.clip(dist, -REL_MAX, REL_MAX) + REL_MAX].astype(F32)
    flat = jnp.tile(jnp.pad(g, ((0, 0), (0, 1))), (1, n_q))[:, :n_q * n]
    bias = flat.reshape(heads, n_q, n)[:, :, n_q - 1:]
    q_chunk = ((q0 + jnp.arange(n_q)) // CHUNK)[:, None]
    k_chunk = ((k0 + jnp.arange(n_k)) // CHUNK)[None, :]
    mask = (jnp.arange(n_k) < n_valid)[None, :] & (k_chunk <= q_chunk) & (k_chunk >= q_chunk - BAND_CHUNKS)
    return jnp.where(mask[None], bias, -jnp.inf)


def _band_attention(q, k, v, tile, heads, tq, tk, prompt):
    b, t_q, d = q.shape
    assert 2 * (d // heads) == 128
    pairs = 4
    w = 128 * pairs
    if prompt:
        kmap = lambda t: (lambda hp, b_, i: (b_, jnp.maximum(i - 2 + t, 0), hp))
    else:
        kmap = lambda t: (lambda hp, b_, i: (b_, t, hp))
    kspecs = [pl.BlockSpec((None, tk, w), kmap(t)) for t in range(3)]
    vspecs = [pl.BlockSpec((None, tk, 2 * w), kmap(t)) for t in range(3)]
    return pl.pallas_call(
        functools.partial(_band_kernel, tk=tk, pairs=pairs, prompt=prompt),
        grid=(heads // (2 * pairs), b, t_q // tq),
        in_specs=[pl.BlockSpec((None, tq, w), lambda hp, b_, i: (b_, i, hp))] + kspecs + vspecs
        + [pl.BlockSpec((2 * pairs, tq, 3 * tk), lambda hp, b_, i: (hp, 0, 0))],
        out_specs=pl.BlockSpec((None, tq, w), lambda hp, b_, i: (b_, i, hp)),
        out_shape=jax.ShapeDtypeStruct((b, t_q, d), BF16),
        compiler_params=_cparams(3),
        name="band_attention",
    )(q, k, k, k, v, v, v, tile)


def _mla_proj_kernel(x_ref, g_ref, wdq_ref, qn_ref, wqc_ref, wqs_ref, wdkv_ref, kvn_ref, wuk_ref, wuv_ref,
                     cos_ref, sin_ref, qc_ref, kc_ref, v_ref, ckv_ref, kr_ref, *, heads, scale):
    w = MLA_NOPE
    h = _rms(x_ref[...], g_ref[...]).astype(BF16)
    qn = _rms(_dot(h, wdq_ref[...]), qn_ref[...]).astype(BF16)
    a = _dot(qn, wqc_ref[...])
    sw = _dot(qn, wqs_ref[...])
    cos = cos_ref[...]
    sin = sin_ref[...]
    for hd in range(heads):
        qc_ref[:, 2 * w * hd:2 * w * hd + w] = (a[:, 2 * w * hd:2 * w * hd + w] * scale).astype(BF16)
        rope = a[:, 2 * w * hd + w:2 * w * (hd + 1)] * cos + sw[:, w * hd:w * (hd + 1)] * sin
        qc_ref[:, 2 * w * hd + w:2 * w * (hd + 1)] = (rope * scale).astype(BF16)
    kv = _dot(h, wdkv_ref[...])
    lora = kv.shape[1] - 2 * w
    ckv = _rms(kv[:, :lora], kvn_ref[...])
    ckv_ref[...] = ckv
    kr = kv[:, lora:lora + w] * cos + kv[:, lora + w:lora + 2 * w] * sin
    kr_ref[...] = kr[:, :MLA_ROPE]
    cb = ckv.astype(BF16)
    kn = _dot(cb, wuk_ref[...])
    vv = _dot(cb, wuv_ref[...])
    krb = kr.astype(BF16)
    ones = jnp.ones((krb.shape[0], w), BF16)
    for hd in range(heads):
        kc_ref[:, 2 * w * hd:2 * w * hd + w] = kn[:, w * hd:w * (hd + 1)].astype(BF16)
        kc_ref[:, 2 * w * hd + w:2 * w * (hd + 1)] = krb
        v_ref[:, 2 * w * hd:2 * w * hd + w] = vv[:, w * hd:w * (hd + 1)].astype(BF16)
        v_ref[:, 2 * w * hd + w:2 * w * (hd + 1)] = ones


def _mla_proj(x2d, g, wts, cos_t, sin_t, heads, scale, tm):
    n, d = x2d.shape
    wdq, qn, wqc, wqs, wdkv, kvn, wuk, wuv = wts
    lora = wuk.shape[0]
    n_pos = cos_t.shape[0] // tm
    row = lambda i: (i, 0)
    pos = lambda i: (i % n_pos, 0)
    return pl.pallas_call(
        functools.partial(_mla_proj_kernel, heads=heads, scale=scale),
        grid=(n // tm,),
        in_specs=[pl.BlockSpec((tm, d), row), _const_spec((1, d)), _const_spec(wdq.shape), _const_spec(qn.shape),
                  _const_spec(wqc.shape), _const_spec(wqs.shape), _const_spec(wdkv.shape), _const_spec(kvn.shape),
                  _const_spec(wuk.shape), _const_spec(wuv.shape),
                  pl.BlockSpec((tm, MLA_NOPE), pos), pl.BlockSpec((tm, MLA_NOPE), pos)],
        out_specs=[pl.BlockSpec((tm, heads * 2 * MLA_NOPE), row), pl.BlockSpec((tm, heads * 2 * MLA_NOPE), row),
                   pl.BlockSpec((tm, heads * 2 * MLA_V), row), pl.BlockSpec((tm, lora), row),
                   pl.BlockSpec((tm, MLA_ROPE), row)],
        out_shape=[jax.ShapeDtypeStruct((n, heads * 2 * MLA_NOPE), BF16),
                   jax.ShapeDtypeStruct((n, heads * 2 * MLA_NOPE), BF16),
                   jax.ShapeDtypeStruct((n, heads * 2 * MLA_V), BF16),
                   jax.ShapeDtypeStruct((n, lora), F32),
                   jax.ShapeDtypeStruct((n, MLA_ROPE), F32)],
        compiler_params=_cparams(1),
        name="mla_proj",
    )(x2d, g.reshape(1, d), wdq, qn, wqc, wqs, wdkv, kvn, wuk, wuv, cos_t, sin_t)


def _mla_attn_kernel(q_ref, kt_ref, v_ref, o_ref, m_ref, alpha_ref, p_ref, acc_ref, *, tq, tk, group, q_off, kv_len,
                     n_mask, mask_all):
    i = pl.program_id(2)
    q_start = q_off + i * tq
    q_chunk = (q_start + lax.broadcasted_iota(jnp.int32, (tq, 1), 0)) // CHUNK
    col = lax.broadcasted_iota(jnp.int32, (1, tk), 1)
    k_end = jnp.minimum(((q_start + tq - 1) // CHUNK + 1) * CHUNK, kv_len)
    n_blk = (k_end + tk - 1) // tk
    m_ref[...] = jnp.full(m_ref.shape, -jnp.inf, F32)
    acc_ref[...] = jnp.zeros(acc_ref.shape, F32)
    strip = min(tq, 64)

    n_rows = 2 if tq >= 256 else 1
    rh = tq // n_rows

    def process(blocks):
        starts = [pl.multiple_of(j * tk, tk) for j, _ in blocks]
        scores = [[_dot(q_ref[pl.ds(h * rh, rh), :], kt_ref[:, pl.ds(st, tk)]) for h in range(n_rows)]
                  for st in starts]
        for n, ((_, masked), st, s_halves) in enumerate(zip(blocks, starts, scores)):
            c = n % 2
            for r0 in range(0, tq, strip):
                rows = pl.ds(r0, strip)
                s = s_halves[r0 // rh][r0 % rh:r0 % rh + strip, :]
                if masked:
                    k_pos = st + col
                    s = jnp.where(((k_pos // CHUNK) <= q_chunk[r0:r0 + strip]) & (k_pos < kv_len), s, -jnp.inf)
                m_prev = m_ref[c, rows, :]
                m_new = jnp.maximum(m_prev, jnp.max(_lane_tiles(s, jnp.maximum), axis=-1, keepdims=True))
                p_ref[n, rows, :] = jnp.exp2(s - jnp.concatenate([m_new] * (tk // 128), axis=1)).astype(BF16)
                alpha_ref[n, rows, :] = jnp.exp2(m_prev - m_new)
                m_ref[c, rows, :] = m_new
        for n, st in enumerate(starts):
            c = n % 2
            for h in range(n_rows):
                rows = pl.ds(h * rh, rh)
                alpha = alpha_ref[n, rows, :]
                acc_ref[c, rows, :] = (jnp.concatenate([alpha, alpha], axis=1) * acc_ref[c, rows, :]
                                       + _dot(p_ref[n, rows, :], v_ref[pl.ds(st, tk), :]))

    def body(g, carry):
        process([(group * g + n, mask_all) for n in range(group)])
        return carry

    n_old = n_blk - n_mask
    n_groups = n_old // group
    lax.fori_loop(0, n_groups, body, 0)
    first = group * n_groups
    for rest in range(group):
        @pl.when(n_old - first == rest)
        def _(rest=rest):
            process([(first + n, mask_all) for n in range(rest)]
                    + [(first + rest + n, True) for n in range(n_mask)])

    m0, m1 = m_ref[0], m_ref[1]
    m = jnp.maximum(m0, m1)
    a0 = jnp.exp2(m0 - m)
    a1 = jnp.exp2(m1 - m)
    acc = jnp.concatenate([a0, a0], axis=1) * acc_ref[0] + jnp.concatenate([a1, a1], axis=1) * acc_ref[1]
    o_ref[...] = (acc[:, :MLA_V] / acc[:, MLA_V:]).astype(o_ref.dtype)


def _mla_attention(qc, kc, v1, heads, tq, tk, q_off, kv_len, group=4):
    b, t_q, _ = qc.shape
    t_k = kc.shape[1]
    w = 2 * MLA_NOPE
    mask_all = not (q_off == 0 and (tq % tk == 0 or tk % tq == 0) and t_q == t_k == kv_len)
    n_mask = 1 if mask_all else max(1, tq // tk)
    slots = group - 1 + n_mask
    kt = jnp.transpose(kc, (0, 2, 1))
    return pl.pallas_call(
        functools.partial(_mla_attn_kernel, tq=tq, tk=tk, group=group, q_off=q_off, kv_len=kv_len,
                          n_mask=n_mask, mask_all=mask_all),
        grid=(b, heads, t_q // tq),
        in_specs=[pl.BlockSpec((None, tq, w), lambda b_, h, i: (b_, i, h)),
                  pl.BlockSpec((None, w, t_k), lambda b_, h, i: (b_, h, 0)),
                  pl.BlockSpec((None, t_k, MLA_V + 128), lambda b_, h, i: (b_, 0, h))],
        out_specs=pl.BlockSpec((None, tq, MLA_V), lambda b_, h, i: (b_, i, h)),
        out_shape=jax.ShapeDtypeStruct((b, t_q, heads * MLA_V), BF16),
        scratch_shapes=[pltpu.VMEM((2, tq, 128), F32), pltpu.VMEM((slots, tq, 128), F32),
                        pltpu.VMEM((slots, tq, tk), BF16),
                        pltpu.VMEM((2, tq, MLA_V + 128), F32)],
        compiler_params=_cparams(3),
        name="mla_attention",
    )(qc, kt, v1)


def _mla_sample_kernel(q_ref, cc_ref, cr_ref, nc_ref, nr_ref, wuk_ref, wuv_ref, o_ref, c_all, r_all, *,
                       heads, past, dec):
    w = MLA_NOPE
    t_k = c_all.shape[0]
    c_all[0:past, :] = cc_ref[...].astype(BF16)
    c_all[past:past + dec, :] = nc_ref[...].astype(BF16)
    c_all[past + dec:, :] = jnp.zeros((t_k - past - dec, c_all.shape[1]), BF16)
    r_all[0:past, :] = cr_ref[...].astype(BF16)
    r_all[past:past + dec, :] = nr_ref[...].astype(BF16)
    r_all[past + dec:, :] = jnp.zeros((t_k - past - dec, r_all.shape[1]), BF16)
    q_lat = jnp.concatenate([_dot_nt(q_ref[:, 2 * w * hd:2 * w * hd + w],
                                     wuk_ref[:, w * hd:w * (hd + 1)]).astype(BF16) for hd in range(heads)], axis=0)
    q_rope = jnp.concatenate([q_ref[:, 2 * w * hd + w:2 * w * hd + w + MLA_ROPE] for hd in range(heads)], axis=0)
    k_pos = lax.broadcasted_iota(jnp.int32, (1, t_k), 1)
    q_chunk = (past + lax.broadcasted_iota(jnp.int32, (heads * dec, 1), 0) % dec) // CHUNK
    s = _dot_nt(q_lat, c_all[...]) + _dot_nt(q_rope, r_all[...])
    s = jnp.where(((k_pos // CHUNK) <= q_chunk) & (k_pos < past + dec), s, -jnp.inf)
    p = jnp.exp2(s - jnp.max(s, axis=-1, keepdims=True))
    o_lat = (_dot(p.astype(BF16), c_all[...]) / jnp.sum(p, axis=-1, keepdims=True)).astype(BF16)
    for hd in range(heads):
        o_ref[:, w * hd:w * (hd + 1)] = _dot(o_lat[hd * dec:(hd + 1) * dec, :],
                                             wuv_ref[:, w * hd:w * (hd + 1)]).astype(o_ref.dtype)


def _mla_sample_attention(qc, cache_ckv, cache_krope, layer, ckv_new, kr_new, wuk, wuv, heads):
    b, dec, _ = qc.shape
    past, lora = cache_ckv.shape[2], cache_ckv.shape[3]
    t_k = -(-(past + dec) // 128) * 128
    cache = lambda width: pl.BlockSpec((None, None, past, width), lambda i: (layer, i, 0, 0))
    new = lambda width: pl.BlockSpec((None, dec, width), lambda i: (i, 0, 0))
    return pl.pallas_call(
        functools.partial(_mla_sample_kernel, heads=heads, past=past, dec=dec),
        grid=(b,),
        in_specs=[new(qc.shape[2]), cache(lora), cache(MLA_ROPE), new(lora), new(MLA_ROPE),
                  _const_spec(wuk.shape), _const_spec(wuv.shape)],
        out_specs=new(heads * MLA_V),
        out_shape=jax.ShapeDtypeStruct((b, dec, heads * MLA_V), BF16),
        scratch_shapes=[pltpu.VMEM((t_k, lora), BF16), pltpu.VMEM((t_k, MLA_ROPE), BF16)],
        compiler_params=_cparams(1),
        name="mla_sample_attention",
    )(qc, cache_ckv, cache_krope, ckv_new, kr_new, wuk, wuv)


def _rope_tables(pos):
    half = MLA_ROPE // 2
    inv_freq = ROPE_THETA ** (-jnp.arange(half, dtype=F32) / half)
    ang = pos.astype(F32)[:, None] * inv_freq[None, :]
    zeros = jnp.zeros((pos.shape[0], MLA_NOPE - MLA_ROPE), F32)
    cos, sin = jnp.cos(ang), jnp.sin(ang)
    return jnp.concatenate([cos, cos, zeros], axis=1), jnp.concatenate([sin, sin, zeros], axis=1)


def _swap_halves(w):
    half = w.shape[-1] // 2
    return jnp.concatenate([-w[..., half:], w[..., :half]], axis=-1)


def _mla_weights(w_dq, q_norm, w_uq, w_dkv, kv_norm, w_uk, w_uv, heads):
    q_lora = w_dq.shape[1]
    lora = w_uk.shape[0]
    pad = MLA_NOPE - MLA_ROPE
    uq = w_uq.reshape(q_lora, heads, MLA_NOPE + MLA_ROPE)
    zq = jnp.zeros((q_lora, heads, pad), F32)
    wqc = jnp.concatenate([uq, zq], axis=-1).reshape(q_lora, heads * 2 * MLA_NOPE)
    wqs = jnp.concatenate([_swap_halves(uq[..., MLA_NOPE:]), zq], axis=-1).reshape(q_lora, heads * MLA_NOPE)
    zk = jnp.zeros((w_dkv.shape[0], pad), F32)
    rope = w_dkv[:, lora:]
    wdkv = jnp.concatenate([w_dkv[:, :lora], rope, zk, _swap_halves(rope), zk], axis=1)
    return (w_dq.astype(BF16), q_norm.reshape(1, q_lora), wqc.astype(BF16), wqs.astype(BF16), wdkv.astype(BF16),
            kv_norm.reshape(1, lora), w_uk.reshape(lora, heads * MLA_NOPE).astype(BF16),
            w_uv.reshape(lora, heads * MLA_V).astype(BF16))


def _pad_rows(x, rows):
    return jnp.pad(x, ((0, 0), (0, rows - x.shape[1]), (0, 0)))


def kernel(x_prompt, x_sample, cache_sb_k, cache_sb_v, cache_band_k, cache_band_v, cache_mla_ckv, cache_mla_krope, norm_mix, norm_ffn, norm_final, w_gate_up, w_down, sb_w_qkv, sb_w_o, band_w_qkv, band_rel_bias, band_w_o, mla_w_dq, mla_q_norm, mla_w_uq, mla_w_dkv, mla_kv_norm, mla_w_uk, mla_w_uv, mla_w_o):
    bp, seq, d = x_prompt.shape
    bs, dec, _ = x_sample.shape
    depth = norm_mix.shape[0]
    past = cache_sb_k.shape[2]
    sb_heads, sb_dh = cache_sb_k.shape[3], cache_sb_k.shape[4]
    band_heads, band_dh = cache_band_k.shape[3], cache_band_k.shape[4]
    band_rows = cache_band_k.shape[2]
    mla_heads = mla_w_uk.shape[2]
    lora = mla_w_uk.shape[1]
    mla_scale = (MLA_NOPE + MLA_ROPE) ** -0.5 * LOG2E

    tm_p, tm_s = 512, bs * dec
    tm_ffn = 512
    blk = 256
    mla_blk = 512
    fc = 256

    xp = x_prompt.reshape(bp * seq, d)
    xs = x_sample.reshape(bs * dec, d)
    outs = {name: [] for name in ("sb_ks", "sb_vs", "band_kp", "band_vp", "band_ks", "band_vs",
                                  "mla_cp", "mla_rp", "mla_cs", "mla_rs")}
    n_sb = sb_w_qkv.shape[0]
    sb_kp = sb_vp = None

    for layer in range(depth):
        kind, j = layer % 3, layer // 3
        g_mix = norm_mix[layer]
        if kind == 0:
            w_bf = sb_w_qkv[j].astype(BF16)
            scale = sb_dh ** -0.5 * LOG2E
            q, sb_kp, sb_vp, kb, vb = _qkv_proj(xp, g_mix, w_bf, scale, tm_p,
                                                stack=(j, n_sb, sb_heads, sb_kp, sb_vp))
            ap = _sb_attention(q.reshape(bp, seq, d), kb.reshape(bp, seq, d), vb.reshape(bp, seq, d),
                               sb_heads, blk, blk, 0)
            q, k, v, kb, vb = _qkv_proj(xs, g_mix, w_bf, scale, tm_s)
            t_k = -(-(past + dec) // blk) * blk
            k_all, v_all = _sb_keys(cache_sb_k, cache_sb_v, j, kb.reshape(bs, dec, d), vb.reshape(bs, dec, d), t_k)
            a_s = _sb_attention(q.reshape(bs, dec, d), k_all, v_all, sb_heads, dec, blk, past)
            outs["sb_ks"].append(k.reshape(bs, dec, sb_heads, sb_dh))
            outs["sb_vs"].append(v.reshape(bs, dec, sb_heads, sb_dh))
            w_o = sb_w_o[j]
        elif kind == 1:
            w_bf = band_w_qkv[j].astype(BF16)
            scale = band_dh ** -0.5 * LOG2E
            win = BAND_CHUNKS * CHUNK
            q, k, v, kb, vb = _qkv_proj(xp, g_mix, w_bf, scale, tm_p, ones_v=True)
            tile = _band_tile(band_rel_bias[j], win, blk, 0, 3 * blk, 3 * blk) * LOG2E
            ap = _band_attention(q.reshape(bp, seq, d), kb.reshape(bp, seq, d), vb.reshape(bp, seq, 2 * d), tile,
                                 band_heads, blk, blk, True)
            keep = min(win, seq)
            outs["band_kp"].append(k.reshape(bp, seq, d)[:, seq - keep:].reshape(bp, keep, band_heads, band_dh))
            outs["band_vp"].append(v.reshape(bp, seq, d)[:, seq - keep:].reshape(bp, keep, band_heads, band_dh))
            q, k, v, kb, vb = _qkv_proj(xs, g_mix, w_bf, scale, tm_s, ones_v=True)
            n_keys = band_rows + dec
            k_all = _pad_rows(jnp.concatenate([cache_band_k[j].reshape(bs, band_rows, d).astype(BF16),
                                               kb.reshape(bs, dec, d)], axis=1), 3 * blk)
            cache_v = cache_band_v[j].reshape(bs, band_rows, d // 128, 128).astype(BF16)
            cache_v = jnp.concatenate([cache_v, jnp.ones_like(cache_v)], axis=-1).reshape(bs, band_rows, 2 * d)
            v_all = _pad_rows(jnp.concatenate([cache_v, vb.reshape(bs, dec, 2 * d)], axis=1), 3 * blk)
            tile = _band_tile(band_rel_bias[j], past, dec, past - band_rows, 3 * blk, n_keys) * LOG2E
            a_s = _band_attention(q.reshape(bs, dec, d), k_all, v_all, tile, band_heads, dec, blk, False)
            outs["band_ks"].append(k.reshape(bs, dec, band_heads, band_dh))
            outs["band_vs"].append(v.reshape(bs, dec, band_heads, band_dh))
            w_o = band_w_o[j]
        else:
            wts = _mla_weights(mla_w_dq[j], mla_q_norm[j], mla_w_uq[j], mla_w_dkv[j], mla_kv_norm[j],
                               mla_w_uk[j], mla_w_uv[j], mla_heads)
            cos_p, sin_p = _rope_tables(jnp.arange(seq))
            qc, kc, vv, ckv, kr = _mla_proj(xp, g_mix, wts, cos_p, sin_p, mla_heads, mla_scale, tm_p)
            ap = _mla_attention(qc.reshape(bp, seq, -1), kc.reshape(bp, seq, -1), vv.reshape(bp, seq, -1),
                                mla_heads, mla_blk, mla_blk, 0, seq)
            outs["mla_cp"].append(ckv.reshape(bp, seq, lora))
            outs["mla_rp"].append(kr.reshape(bp, seq, MLA_ROPE))
            cos_s, sin_s = _rope_tables(jnp.tile(past + jnp.arange(dec), bs))
            qc, kc, vv, ckv, kr = _mla_proj(xs, g_mix, wts, cos_s, sin_s, mla_heads, mla_scale, tm_s)
            a_s = _mla_sample_attention(qc.reshape(bs, dec, -1), cache_mla_ckv, cache_mla_krope, j,
                                        ckv.reshape(bs, dec, lora), kr.reshape(bs, dec, MLA_ROPE),
                                        wts[6], wts[7], mla_heads)
            outs["mla_cs"].append(ckv.reshape(bs, dec, lora))
            outs["mla_rs"].append(kr.reshape(bs, dec, MLA_ROPE))
            w_o = mla_w_o[j]

        wgu, wd = w_gate_up[layer].astype(BF16), w_down[layer].astype(BF16)
        final = layer == depth - 1
        wo_bf = w_o.astype(BF16)
        xp = _oproj_ffn(xp, ap.reshape(bp * seq, d), wo_bf, norm_ffn[layer], wgu, wd, norm_final, final, tm_ffn, fc)
        xs = _oproj_ffn(xs, a_s.reshape(bs * dec, d), wo_bf, norm_ffn[layer], wgu, wd, norm_final, final, tm_s, fc)

    st = lambda name: jnp.stack(outs[name])
    return (xp.reshape(bp, seq, d), xs.reshape(bs, dec, d),
            sb_kp.reshape(n_sb, bp, seq, sb_heads, sb_dh), sb_vp.reshape(n_sb, bp, seq, sb_heads, sb_dh),
            st("band_kp"), st("band_vp"), st("mla_cp"), st("mla_rp"),
            st("sb_ks"), st("sb_vs"), st("band_ks"), st("band_vs"), st("mla_cs"), st("mla_rs"))
```

```python
import functools

import jax
import jax.numpy as jnp
from jax import lax
from jax.experimental import pallas as pl
from jax.experimental.pallas import tpu as pltpu

CHUNK = 64
BAND_CHUNKS = 8
REL_MAX = 128
MLA_NOPE = 128
MLA_ROPE = 64
MLA_V = 128
ROPE_THETA = 10000.0
EPS = 1e-6

LOG2E = 1.4426950408889634
SB_LOG2_ZERO = -151.0

VMEM_LIMIT = 56 * 1024 * 1024
F32 = jnp.float32
BF16 = jnp.bfloat16


def _cparams(n_grid):
    return pltpu.CompilerParams(dimension_semantics=("arbitrary",) * n_grid, vmem_limit_bytes=VMEM_LIMIT)


def _rms(x, g):
    return x * lax.rsqrt(jnp.mean(x * x, axis=-1, keepdims=True) + EPS) * g


def _dot(a, b):
    return jnp.dot(a, b, preferred_element_type=F32)


def _dot_nt(a, b):
    return lax.dot_general(a, b, (((1,), (1,)), ((), ())), preferred_element_type=F32)


def _lane_tiles(x, op):
    tiles = [x[:, t:t + 128] for t in range(0, x.shape[1], 128)]
    while len(tiles) > 1:
        tiles = [op(tiles[t], tiles[t + 1]) for t in range(0, len(tiles) - 1, 2)] + tiles[len(tiles) & ~1:]
    return tiles[0]


def _const_spec(shape):
    nd = len(shape)
    return pl.BlockSpec(shape, lambda *_: (0,) * nd)


def _qkv_kernel(x_ref, g_ref, w_ref, *refs, d, q_scale, ones_v, heads_out, first_slot):
    q_ref, k_ref, v_ref, kb_ref, vb_ref = refs[-5:]
    h = _rms(x_ref[...], g_ref[...]).astype(BF16)
    q = _dot(h, w_ref[:, 0:d])
    q_ref[...] = (q * q_scale).astype(BF16)
    k = _dot(h, w_ref[:, d:2 * d])
    kb_ref[...] = k.astype(BF16)
    v = _dot(h, w_ref[:, 2 * d:3 * d])
    if heads_out is None:
        k_ref[...] = k
        v_ref[...] = v
    else:
        for o_ref, val in ((k_ref, k), (v_ref, v)):
            val = pltpu.einshape("r(hd)->rhd", val, h=heads_out)
            if first_slot:
                o_ref[0] = val
                o_ref[1:] = jnp.zeros((o_ref.shape[0] - 1,) + val.shape, F32)
            else:
                o_ref[...] = val
    if ones_v:
        ones = jnp.ones((k.shape[0], 128), BF16)
        for t in range(d // 128):
            vb_ref[:, 256 * t:256 * t + 128] = v[:, 128 * t:128 * (t + 1)].astype(BF16)
            vb_ref[:, 256 * t + 128:256 * (t + 1)] = ones
    else:
        vb_ref[...] = v.astype(BF16)


def _qkv_proj(x2d, g, w_bf, q_scale, tm, ones_v=False, stack=None):
    n, d = x2d.shape
    row = lambda i: (i, 0)
    wv = 2 * d if ones_v else d
    in_specs = [pl.BlockSpec((tm, d), row), _const_spec((1, d)), _const_spec((d, 3 * d))]
    args = [x2d, g.reshape(1, d), w_bf]
    heads_out, first_slot, aliases = None, False, {}
    kv_specs = [pl.BlockSpec((tm, d), row)] * 2
    kv_shapes = [jax.ShapeDtypeStruct((n, d), F32)] * 2
    if stack is not None:
        slot, n_slots, heads_out, k_buf, v_buf = stack
        first_slot = slot == 0
        kv_shapes = [jax.ShapeDtypeStruct((n_slots, n, heads_out, d // heads_out), F32)] * 2
        if first_slot:
            kv_specs = [pl.BlockSpec((n_slots, tm, heads_out, d // heads_out), lambda i: (0, i, 0, 0))] * 2
        else:
            kv_specs = [pl.BlockSpec((None, tm, heads_out, d // heads_out), lambda i: (slot, i, 0, 0))] * 2
            in_specs += [pl.BlockSpec(memory_space=pl.ANY)] * 2
            args += [k_buf, v_buf]
            aliases = {3: 1, 4: 2}
    return pl.pallas_call(
        functools.partial(_qkv_kernel, d=d, q_scale=q_scale, ones_v=ones_v, heads_out=heads_out,
                          first_slot=first_slot),
        grid=(n // tm,),
        in_specs=in_specs,
        out_specs=[pl.BlockSpec((tm, d), row)] + kv_specs + [pl.BlockSpec((tm, d), row), pl.BlockSpec((tm, wv), row)],
        out_shape=[jax.ShapeDtypeStruct((n, d), BF16)] + kv_shapes
        + [jax.ShapeDtypeStruct((n, d), BF16), jax.ShapeDtypeStruct((n, wv), BF16)],
        input_output_aliases=aliases,
        compiler_params=_cparams(1),
        name="qkv_proj",
    )(*args)


def _ffn_kernel(x_ref, a_ref, wo_ref, g_ref, wgu_ref, wd_ref, gf_ref, o_ref, acc_ref, h_ref, *, n_chunks, fc, final):
    x1 = x_ref[...] + _dot(a_ref[...], wo_ref[...])
    h_ref[...] = _rms(x1, g_ref[...]).astype(BF16)
    acc_ref[...] = x1

    def body(c, carry):
        lo = pl.multiple_of(c * fc, fc)
        gate = _dot(h_ref[...], wgu_ref[:, pl.ds(lo, fc)])
        up = _dot(h_ref[...], wgu_ref[:, pl.ds(pl.multiple_of(n_chunks * fc + lo, fc), fc)])
        act = (gate / (1.0 + jnp.exp(-gate)) * up).astype(BF16)
        acc_ref[...] += _dot(act, wd_ref[pl.ds(lo, fc), :])
        return carry

    lax.fori_loop(0, n_chunks, body, 0, unroll=True)
    out = acc_ref[...]
    if final:
        out = _rms(out, gf_ref[...])
    o_ref[...] = out


def _oproj_ffn(x2d, a2d, wo_bf, g, wgu_c, wd_c, g_final, final, tm, fc):
    n, d = x2d.shape
    n_chunks = wd_c.shape[0] // fc
    row = lambda i: (i, 0)
    return pl.pallas_call(
        functools.partial(_ffn_kernel, n_chunks=n_chunks, fc=fc, final=final),
        grid=(n // tm,),
        in_specs=[pl.BlockSpec((tm, d), row), pl.BlockSpec((tm, d), row), _const_spec((d, d)), _const_spec((1, d)),
                  _const_spec(wgu_c.shape), _const_spec(wd_c.shape), _const_spec((1, d))],
        out_specs=pl.BlockSpec((tm, d), row),
        out_shape=jax.ShapeDtypeStruct((n, d), F32),
        scratch_shapes=[pltpu.VMEM((tm, d), F32), pltpu.VMEM((tm, d), BF16)],
        compiler_params=_cparams(1),
        name="oproj_ffn",
    )(x2d, a2d, wo_bf, g.reshape(1, d), wgu_c, wd_c, g_final.reshape(1, d))


def _sb_kernel(q_ref, k_ref, v_ref, tri_ref, o_ref, a_ref, acc_ref, *, tq, tk, dh, n_heads, q_off):
    i = pl.program_id(2)
    q_start = q_off + i * tq
    q_pos = q_start + lax.broadcasted_iota(jnp.int32, (tq, 1), 0)
    col = lax.broadcasted_iota(jnp.int32, (1, tk), 1)
    j_last = (q_start + tq - 2) // tk
    a_ref[...] = jnp.zeros(a_ref.shape, F32)
    acc_ref[...] = jnp.zeros(acc_ref.shape, F32)

    def cond(state):
        j, a_max = state
        return jnp.logical_and(j >= 0, a_max > SB_LOG2_ZERO)

    def body(state):
        j, _ = state
        tri = tri_ref[...]
        lanes = [pl.ds(c * dh, dh) for c in range(n_heads)]
        starts = [pl.multiple_of(j * tk, tk), pl.multiple_of(jnp.maximum(j - 1, 0) * tk, tk)]
        masks = [(starts[0] + col) < q_pos, jnp.logical_and((starts[1] + col) < q_pos, j >= 1)]
        zs = [[jnp.where(mask, _dot_nt(q_ref[:, ln], k_ref[pl.ds(st, tk), ln]), -jnp.inf) for ln in lanes]
              for st, mask in zip(starts, masks)]
        softs = [[jnp.maximum(z, 0.0) + jnp.log(1.0 + jnp.exp2(-jnp.abs(z))) * LOG2E for z in zb] for zb in zs]
        rs = []
        for sb in softs:
            rb = []
            for soft in sb:
                rb.append(_dot(soft.astype(BF16), tri))
            rs.append(rb)
        a_max = jnp.float32(-jnp.inf)
        ws = [[], []]
        for c in range(n_heads):
            a = a_ref[c]
            for blk in range(2):
                a_wide = jnp.concatenate([a] * (tk // 128), axis=1)
                ws[blk].append(jnp.exp2(zs[blk][c] + rs[blk][c] + a_wide).astype(BF16))
                a = a - jnp.sum(_lane_tiles(softs[blk][c], jnp.add), axis=-1, keepdims=True)
            a_ref[c] = a
            a_max = jnp.maximum(a_max, jnp.max(a))
        for c in range(n_heads):
            acc_ref[:, lanes[c]] += (_dot(ws[0][c], v_ref[pl.ds(starts[0], tk), lanes[c]])
                                     + _dot(ws[1][c], v_ref[pl.ds(starts[1], tk), lanes[c]]))
        return j - 2, a_max

    lax.while_loop(cond, body, (j_last, jnp.float32(0.0)))
    o_ref[...] = acc_ref[...].astype(o_ref.dtype)


def _sb_keys_kernel(ck_ref, cv_ref, nk_ref, nv_ref, k_ref, v_ref, *, heads, past, dec):
    dh = ck_ref.shape[2]
    for c_ref, n_ref, o_ref in ((ck_ref, nk_ref, k_ref), (cv_ref, nv_ref, v_ref)):
        by_head = pltpu.einshape("rhd->hrd", c_ref[...])
        for hd in range(heads):
            o_ref[0:past, hd * dh:(hd + 1) * dh] = by_head[hd].astype(BF16)
        o_ref[past:past + dec, :] = n_ref[...]
        o_ref[past + dec:, :] = jnp.zeros((o_ref.shape[0] - past - dec, o_ref.shape[1]), BF16)


def _sb_keys(cache_k, cache_v, layer, new_k, new_v, t_k):
    _, b, past, heads, dh = cache_k.shape
    dec, d = new_k.shape[1], new_k.shape[2]
    cache_spec = pl.BlockSpec((None, None, past, heads, dh), lambda i: (layer, i, 0, 0, 0))
    new_spec = pl.BlockSpec((None, dec, d), lambda i: (i, 0, 0))
    out_spec = pl.BlockSpec((None, t_k, d), lambda i: (i, 0, 0))
    return pl.pallas_call(
        functools.partial(_sb_keys_kernel, heads=heads, past=past, dec=dec),
        grid=(b,),
        in_specs=[cache_spec, cache_spec, new_spec, new_spec],
        out_specs=[out_spec, out_spec],
        out_shape=[jax.ShapeDtypeStruct((b, t_k, d), BF16)] * 2,
        compiler_params=_cparams(1),
        name="sb_keys",
    )(cache_k, cache_v, new_k, new_v)


def _sb_attention(q, k, v, heads, tq, tk, q_off):
    b, t_q, d = q.shape
    t_k = k.shape[1]
    dh = d // heads
    n_heads = 2
    tri = -jnp.tril(jnp.ones((tk, tk), F32)).astype(BF16)
    return pl.pallas_call(
        functools.partial(_sb_kernel, tq=tq, tk=tk, dh=dh, n_heads=n_heads, q_off=q_off),
        grid=(b, heads // n_heads, t_q // tq),
        in_specs=[pl.BlockSpec((None, tq, n_heads * dh), lambda b_, h, i: (b_, i, h)),
                  pl.BlockSpec((None, t_k, n_heads * dh), lambda b_, h, i: (b_, 0, h)),
                  pl.BlockSpec((None, t_k, n_heads * dh), lambda b_, h, i: (b_, 0, h)),
                  _const_spec((tk, tk))],
        out_specs=pl.BlockSpec((None, tq, n_heads * dh), lambda b_, h, i: (b_, i, h)),
        out_shape=jax.ShapeDtypeStruct((b, t_q, d), BF16),
        scratch_shapes=[pltpu.VMEM((n_heads, tq, 128), F32), pltpu.VMEM((tq, n_heads * dh), F32)],
        compiler_params=_cparams(3),
        name="sb_attention",
    )(q, k, v, tri)


def _band_kernel(q_ref, k0_ref, k1_ref, k2_ref, v0_ref, v1_ref, v2_ref, tile_ref, o_ref, *, tk, pairs, prompt):
    tq = q_ref.shape[0]
    pair = 128
    half = pair // 2
    lane = lax.broadcasted_iota(jnp.int32, (1, pair), 1)
    col = lax.broadcasted_iota(jnp.int32, (1, 3 * tk), 1)

    def compute(first_col):
        scores = []
        for c in range(pairs):
            ln = pl.ds(c * pair, pair)
            q = q_ref[:, ln]
            kw = jnp.concatenate([k0_ref[:, ln], k1_ref[:, ln], k2_ref[:, ln]], axis=0)
            for hh in range(2):
                in_head = (lane >= hh * half) & (lane < (hh + 1) * half)
                s = _dot_nt(jnp.where(in_head, q, jnp.zeros_like(q)), kw) + tile_ref[2 * c + hh]
                if first_col is not None:
                    s = jnp.where(col >= first_col, s, -jnp.inf)
                scores.append(s)
        weights = []
        for s in scores:
            m = jnp.broadcast_to(jnp.max(_lane_tiles(s, jnp.maximum), axis=-1, keepdims=True), (tq, 128))
            weights.append(jnp.exp2(s - jnp.concatenate([m] * (3 * tk // 128), axis=1)).astype(BF16))
        for c in range(pairs):
            lv = pl.ds(c * 2 * pair, 2 * pair)
            vw = jnp.concatenate([v0_ref[:, lv], v1_ref[:, lv], v2_ref[:, lv]], axis=0)
            outs = []
            for hh in range(2):
                r = _dot(weights[2 * c + hh], vw)
                outs.append(r[:, :pair] / r[:, pair:])
            o_ref[:, pl.ds(c * pair, pair)] = jnp.where(lane < half, outs[0], outs[1]).astype(o_ref.dtype)

    if prompt:
        i = pl.program_id(2)

        @pl.when(i < 2)
        def _():
            compute((2 - i) * tk)

        @pl.when(i >= 2)
        def _():
            compute(None)
    else:
        compute(None)


def _band_tile(rel_bias, q0, n_q, k0, n_k, n_valid):
    heads = rel_bias.shape[0]
    n = n_q + n_k - 1
    dist = (q0 - k0) + (n_q - 1) - jnp.arange(n)
    g = rel_bias[:, jnp.clip(dist, -REL_MAX, REL_MAX) + REL_MAX].astype(F32)
    flat = jnp.tile(jnp.pad(g, ((0, 0), (0, 1))), (1, n_q))[:, :n_q * n]
    bias = flat.reshape(heads, n_q, n)[:, :, n_q - 1:]
    q_chunk = ((q0 + jnp.arange(n_q)) // CHUNK)[:, None]
    k_chunk = ((k0 + jnp.arange(n_k)) // CHUNK)[None, :]
    mask = (jnp.arange(n_k) < n_valid)[None, :] & (k_chunk <= q_chunk) & (k_chunk >= q_chunk - BAND_CHUNKS)
    return jnp.where(mask[None], bias, -jnp.inf)


def _band_attention(q, k, v, tile, heads, tq, tk, prompt):
    b, t_q, d = q.shape
    assert 2 * (d // heads) == 128
    pairs = 8
    w = 128 * pairs
    if prompt:
        kmap = lambda t: (lambda hp, b_, i: (b_, jnp.maximum(i - 2 + t, 0), hp))
    else:
        kmap = lambda t: (lambda hp, b_, i: (b_, t, hp))
    kspecs = [pl.BlockSpec((None, tk, w), kmap(t)) for t in range(3)]
    vspecs = [pl.BlockSpec((None, tk, 2 * w), kmap(t)) for t in range(3)]
    return pl.pallas_call(
        functools.partial(_band_kernel, tk=tk, pairs=pairs, prompt=prompt),
        grid=(heads // (2 * pairs), b, t_q // tq),
        in_specs=[pl.BlockSpec((None, tq, w), lambda hp, b_, i: (b_, i, hp))] + kspecs + vspecs
        + [pl.BlockSpec((2 * pairs, tq, 3 * tk), lambda hp, b_, i: (hp, 0, 0))],
        out_specs=pl.BlockSpec((None, tq, w), lambda hp, b_, i: (b_, i, hp)),
        out_shape=jax.ShapeDtypeStruct((b, t_q, d), BF16),
        compiler_params=_cparams(3),
        name="band_attention",
    )(q, k, k, k, v, v, v, tile)


def _mla_proj_kernel(x_ref, g_ref, wdq_ref, qn_ref, wqc_ref, wqs_ref, wdkv_ref, kvn_ref, wuk_ref, wuv_ref,
                     cos_ref, sin_ref, qc_ref, kc_ref, v_ref, ckv_ref, kr_ref, *, heads, scale):
    w = MLA_NOPE
    h = _rms(x_ref[...], g_ref[...]).astype(BF16)
    qn = _rms(_dot(h, wdq_ref[...]), qn_ref[...]).astype(BF16)
    a = _dot(qn, wqc_ref[...])
    sw = _dot(qn, wqs_ref[...])
    cos = cos_ref[...]
    sin = sin_ref[...]
    for hd in range(heads):
        qc_ref[:, 2 * w * hd:2 * w * hd + w] = (a[:, 2 * w * hd:2 * w * hd + w] * scale).astype(BF16)
        rope = a[:, 2 * w * hd + w:2 * w * (hd + 1)] * cos + sw[:, w * hd:w * (hd + 1)] * sin
        qc_ref[:, 2 * w * hd + w:2 * w * (hd + 1)] = (rope * scale).astype(BF16)
    kv = _dot(h, wdkv_ref[...])
    lora = kv.shape[1] - 2 * w
    ckv = _rms(kv[:, :lora], kvn_ref[...])
    ckv_ref[...] = ckv
    kr = kv[:, lora:lora + w] * cos + kv[:, lora + w:lora + 2 * w] * sin
    kr_ref[...] = kr[:, :MLA_ROPE]
    cb = ckv.astype(BF16)
    kn = _dot(cb, wuk_ref[...])
    vv = _dot(cb, wuv_ref[...])
    krb = kr.astype(BF16)
    ones = jnp.ones((krb.shape[0], w), BF16)
    for hd in range(heads):
        kc_ref[:, 2 * w * hd:2 * w * hd + w] = kn[:, w * hd:w * (hd + 1)].astype(BF16)
        kc_ref[:, 2 * w * hd + w:2 * w * (hd + 1)] = krb
        v_ref[:, 2 * w * hd:2 * w * hd + w] = vv[:, w * hd:w * (hd + 1)].astype(BF16)
        v_ref[:, 2 * w * hd + w:2 * w * (hd + 1)] = ones


def _mla_proj(x2d, g, wts, cos_t, sin_t, heads, scale, tm):
    n, d = x2d.shape
    wdq, qn, wqc, wqs, wdkv, kvn, wuk, wuv = wts
    lora = wuk.shape[0]
    n_pos = cos_t.shape[0] // tm
    row = lambda i: (i, 0)
    pos = lambda i: (i % n_pos, 0)
    return pl.pallas_call(
        functools.partial(_mla_proj_kernel, heads=heads, scale=scale),
        grid=(n // tm,),
        in_specs=[pl.BlockSpec((tm, d), row), _const_spec((1, d)), _const_spec(wdq.shape), _const_spec(qn.shape),
                  _const_spec(wqc.shape), _const_spec(wqs.shape), _const_spec(wdkv.shape), _const_spec(kvn.shape),
                  _const_spec(wuk.shape), _const_spec(wuv.shape),
                  pl.BlockSpec((tm, MLA_NOPE), pos), pl.BlockSpec((tm, MLA_NOPE), pos)],
        out_specs=[pl.BlockSpec((tm, heads * 2 * MLA_NOPE), row), pl.BlockSpec((tm, heads * 2 * MLA_NOPE), row),
                   pl.BlockSpec((tm, heads * 2 * MLA_V), row), pl.BlockSpec((tm, lora), row),
                   pl.BlockSpec((tm, MLA_ROPE), row)],
        out_shape=[jax.ShapeDtypeStruct((n, heads * 2 * MLA_NOPE), BF16),
                   jax.ShapeDtypeStruct((n, heads * 2 * MLA_NOPE), BF16),
                   jax.ShapeDtypeStruct((n, heads * 2 * MLA_V), BF16),
                   jax.ShapeDtypeStruct((n, lora), F32),
                   jax.ShapeDtypeStruct((n, MLA_ROPE), F32)],
        compiler_params=_cparams(1),
        name="mla_proj",
    )(x2d, g.reshape(1, d), wdq, qn, wqc, wqs, wdkv, kvn, wuk, wuv, cos_t, sin_t)


def _mla_attn_kernel(q_ref, kt_ref, v_ref, o_ref, m_ref, alpha_ref, p_ref, acc_ref, *, tq, tk, group, q_off, kv_len,
                     n_mask, mask_all):
    i = pl.program_id(2)
    q_start = q_off + i * tq
    q_chunk = (q_start + lax.broadcasted_iota(jnp.int32, (tq, 1), 0)) // CHUNK
    col = lax.broadcasted_iota(jnp.int32, (1, tk), 1)
    k_end = jnp.minimum(((q_start + tq - 1) // CHUNK + 1) * CHUNK, kv_len)
    n_blk = (k_end + tk - 1) // tk
    m_ref[...] = jnp.full(m_ref.shape, -jnp.inf, F32)
    acc_ref[...] = jnp.zeros(acc_ref.shape, F32)
    strip = min(tq, 64)

    n_rows = 2 if tq >= 256 else 1
    rh = tq // n_rows

    def process(blocks):
        starts = [pl.multiple_of(j * tk, tk) for j, _ in blocks]
        scores = [[_dot(q_ref[pl.ds(h * rh, rh), :], kt_ref[:, pl.ds(st, tk)]) for h in range(n_rows)]
                  for st in starts]
        for n, ((_, masked), st, s_halves) in enumerate(zip(blocks, starts, scores)):
            c = n % 2
            for r0 in range(0, tq, strip):
                rows = pl.ds(r0, strip)
                s = s_halves[r0 // rh][r0 % rh:r0 % rh + strip, :]
                if masked:
                    k_pos = st + col
                    s = jnp.where(((k_pos // CHUNK) <= q_chunk[r0:r0 + strip]) & (k_pos < kv_len), s, -jnp.inf)
                m_prev = m_ref[c, rows, :]
                m_new = jnp.maximum(m_prev, jnp.max(_lane_tiles(s, jnp.maximum), axis=-1, keepdims=True))
                p_ref[n, rows, :] = jnp.exp2(s - jnp.concatenate([m_new] * (tk // 128), axis=1)).astype(BF16)
                alpha_ref[n, rows, :] = jnp.exp2(m_prev - m_new)
                m_ref[c, rows, :] = m_new
        for n, st in enumerate(starts):
            c = n % 2
            for h in range(n_rows):
                rows = pl.ds(h * rh, rh)
                alpha = alpha_ref[n, rows, :]
                acc_ref[c, rows, :] = (jnp.concatenate([alpha, alpha], axis=1) * acc_ref[c, rows, :]
                                       + _dot(p_ref[n, rows, :], v_ref[pl.ds(st, tk), :]))

    def body(g, carry):
        process([(group * g + n, mask_all) for n in range(group)])
        return carry

    n_old = n_blk - n_mask
    n_groups = n_old // group
    lax.fori_loop(0, n_groups, body, 0)
    first = group * n_groups
    for rest in range(group):
        @pl.when(n_old - first == rest)
        def _(rest=rest):
            process([(first + n, mask_all) for n in range(rest)]
                    + [(first + rest + n, True) for n in range(n_mask)])

    m0, m1 = m_ref[0], m_ref[1]
    m = jnp.maximum(m0, m1)
    a0 = jnp.exp2(m0 - m)
    a1 = jnp.exp2(m1 - m)
    acc = jnp.concatenate([a0, a0], axis=1) * acc_ref[0] + jnp.concatenate([a1, a1], axis=1) * acc_ref[1]
    o_ref[...] = (acc[:, :MLA_V] / acc[:, MLA_V:]).astype(o_ref.dtype)


def _mla_attention(qc, kc, v1, heads, tq, tk, q_off, kv_len, group=4):
    b, t_q, _ = qc.shape
    t_k = kc.shape[1]
    w = 2 * MLA_NOPE
    mask_all = not (q_off == 0 and (tq % tk == 0 or tk % tq == 0) and t_q == t_k == kv_len)
    n_mask = 1 if mask_all else max(1, tq // tk)
    slots = group - 1 + n_mask
    kt = jnp.transpose(kc, (0, 2, 1))
    return pl.pallas_call(
        functools.partial(_mla_attn_kernel, tq=tq, tk=tk, group=group, q_off=q_off, kv_len=kv_len,
                          n_mask=n_mask, mask_all=mask_all),
        grid=(b, heads, t_q // tq),
        in_specs=[pl.BlockSpec((None, tq, w), lambda b_, h, i: (b_, i, h)),
                  pl.BlockSpec((None, w, t_k), lambda b_, h, i: (b_, h, 0)),
                  pl.BlockSpec((None, t_k, MLA_V + 128), lambda b_, h, i: (b_, 0, h))],
        out_specs=pl.BlockSpec((None, tq, MLA_V), lambda b_, h, i: (b_, i, h)),
        out_shape=jax.ShapeDtypeStruct((b, t_q, heads * MLA_V), BF16),
        scratch_shapes=[pltpu.VMEM((2, tq, 128), F32), pltpu.VMEM((slots, tq, 128), F32),
                        pltpu.VMEM((slots, tq, tk), BF16),
                        pltpu.VMEM((2, tq, MLA_V + 128), F32)],
        compiler_params=_cparams(3),
        name="mla_attention",
    )(qc, kt, v1)


def _mla_sample_kernel(q_ref, cc_ref, cr_ref, nc_ref, nr_ref, wuk_ref, wuv_ref, o_ref, c_all, r_all, *,
                       heads, past, dec):
    w = MLA_NOPE
    t_k = c_all.shape[0]
    c_all[0:past, :] = cc_ref[...].astype(BF16)
    c_all[past:past + dec, :] = nc_ref[...].astype(BF16)
    c_all[past + dec:, :] = jnp.zeros((t_k - past - dec, c_all.shape[1]), BF16)
    r_all[0:past, :] = cr_ref[...].astype(BF16)
    r_all[past:past + dec, :] = nr_ref[...].astype(BF16)
    r_all[past + dec:, :] = jnp.zeros((t_k - past - dec, r_all.shape[1]), BF16)
    q_lat = jnp.concatenate([_dot_nt(q_ref[:, 2 * w * hd:2 * w * hd + w],
                                     wuk_ref[:, w * hd:w * (hd + 1)]).astype(BF16) for hd in range(heads)], axis=0)
    q_rope = jnp.concatenate([q_ref[:, 2 * w * hd + w:2 * w * hd + w + MLA_ROPE] for hd in range(heads)], axis=0)
    k_pos = lax.broadcasted_iota(jnp.int32, (1, t_k), 1)
    q_chunk = (past + lax.broadcasted_iota(jnp.int32, (heads * dec, 1), 0) % dec) // CHUNK
    s = _dot_nt(q_lat, c_all[...]) + _dot_nt(q_rope, r_all[...])
    s = jnp.where(((k_pos // CHUNK) <= q_chunk) & (k_pos < past + dec), s, -jnp.inf)
    p = jnp.exp2(s - jnp.max(s, axis=-1, keepdims=True))
    o_lat = (_dot(p.astype(BF16), c_all[...]) / jnp.sum(p, axis=-1, keepdims=True)).astype(BF16)
    for hd in range(heads):
        o_ref[:, w * hd:w * (hd + 1)] = _dot(o_lat[hd * dec:(hd + 1) * dec, :],
                                             wuv_ref[:, w * hd:w * (hd + 1)]).astype(o_ref.dtype)


def _mla_sample_attention(qc, cache_ckv, cache_krope, layer, ckv_new, kr_new, wuk, wuv, heads):
    b, dec, _ = qc.shape
    past, lora = cache_ckv.shape[2], cache_ckv.shape[3]
    t_k = -(-(past + dec) // 128) * 128
    cache = lambda width: pl.BlockSpec((None, None, past, width), lambda i: (layer, i, 0, 0))
    new = lambda width: pl.BlockSpec((None, dec, width), lambda i: (i, 0, 0))
    return pl.pallas_call(
        functools.partial(_mla_sample_kernel, heads=heads, past=past, dec=dec),
        grid=(b,),
        in_specs=[new(qc.shape[2]), cache(lora), cache(MLA_ROPE), new(lora), new(MLA_ROPE),
                  _const_spec(wuk.shape), _const_spec(wuv.shape)],
        out_specs=new(heads * MLA_V),
        out_shape=jax.ShapeDtypeStruct((b, dec, heads * MLA_V), BF16),
        scratch_shapes=[pltpu.VMEM((t_k, lora), BF16), pltpu.VMEM((t_k, MLA_ROPE), BF16)],
        compiler_params=_cparams(1),
        name="mla_sample_attention",
    )(qc, cache_ckv, cache_krope, ckv_new, kr_new, wuk, wuv)


def _rope_tables(pos):
    half = MLA_ROPE // 2
    inv_freq = ROPE_THETA ** (-jnp.arange(half, dtype=F32) / half)
    ang = pos.astype(F32)[:, None] * inv_freq[None, :]
    zeros = jnp.zeros((pos.shape[0], MLA_NOPE - MLA_ROPE), F32)
    cos, sin = jnp.cos(ang), jnp.sin(ang)
    return jnp.concatenate([cos, cos, zeros], axis=1), jnp.concatenate([sin, sin, zeros], axis=1)


def _swap_halves(w):
    half = w.shape[-1] // 2
    return jnp.concatenate([-w[..., half:], w[..., :half]], axis=-1)


def _mla_weights(w_dq, q_norm, w_uq, w_dkv, kv_norm, w_uk, w_uv, heads):
    q_lora = w_dq.shape[1]
    lora = w_uk.shape[0]
    pad = MLA_NOPE - MLA_ROPE
    uq = w_uq.reshape(q_lora, heads, MLA_NOPE + MLA_ROPE)
    zq = jnp.zeros((q_lora, heads, pad), F32)
    wqc = jnp.concatenate([uq, zq], axis=-1).reshape(q_lora, heads * 2 * MLA_NOPE)
    wqs = jnp.concatenate([_swap_halves(uq[..., MLA_NOPE:]), zq], axis=-1).reshape(q_lora, heads * MLA_NOPE)
    zk = jnp.zeros((w_dkv.shape[0], pad), F32)
    rope = w_dkv[:, lora:]
    wdkv = jnp.concatenate([w_dkv[:, :lora], rope, zk, _swap_halves(rope), zk], axis=1)
    return (w_dq.astype(BF16), q_norm.reshape(1, q_lora), wqc.astype(BF16), wqs.astype(BF16), wdkv.astype(BF16),
            kv_norm.reshape(1, lora), w_uk.reshape(lora, heads * MLA_NOPE).astype(BF16),
            w_uv.reshape(lora, heads * MLA_V).astype(BF16))


def _pad_rows(x, rows):
    return jnp.pad(x, ((0, 0), (0, rows - x.shape[1]), (0, 0)))


def kernel(x_prompt, x_sample, cache_sb_k, cache_sb_v, cache_band_k, cache_band_v, cache_mla_ckv, cache_mla_krope, norm_mix, norm_ffn, norm_final, w_gate_up, w_down, sb_w_qkv, sb_w_o, band_w_qkv, band_rel_bias, band_w_o, mla_w_dq, mla_q_norm, mla_w_uq, mla_w_dkv, mla_kv_norm, mla_w_uk, mla_w_uv, mla_w_o):
    bp, seq, d = x_prompt.shape
    bs, dec, _ = x_sample.shape
    depth = norm_mix.shape[0]
    past = cache_sb_k.shape[2]
    sb_heads, sb_dh = cache_sb_k.shape[3], cache_sb_k.shape[4]
    band_heads, band_dh = cache_band_k.shape[3], cache_band_k.shape[4]
    band_rows = cache_band_k.shape[2]
    mla_heads = mla_w_uk.shape[2]
    lora = mla_w_uk.shape[1]
    mla_scale = (MLA_NOPE + MLA_ROPE) ** -0.5 * LOG2E

    tm_p, tm_s = 512, bs * dec
    tm_ffn = 512
    blk = 256
    mla_blk = 512
    fc = 256

    xp = x_prompt.reshape(bp * seq, d)
    xs = x_sample.reshape(bs * dec, d)
    outs = {name: [] for name in ("sb_ks", "sb_vs", "band_kp", "band_vp", "band_ks", "band_vs",
                                  "mla_cp", "mla_rp", "mla_cs", "mla_rs")}
    n_sb = sb_w_qkv.shape[0]
    sb_kp = sb_vp = None

    for layer in range(depth):
        kind, j = layer % 3, layer // 3
        g_mix = norm_mix[layer]
        if kind == 0:
            w_bf = sb_w_qkv[j].astype(BF16)
            scale = sb_dh ** -0.5 * LOG2E
            q, sb_kp, sb_vp, kb, vb = _qkv_proj(xp, g_mix, w_bf, scale, tm_p,
                                                stack=(j, n_sb, sb_heads, sb_kp, sb_vp))
            ap = _sb_attention(q.reshape(bp, seq, d), kb.reshape(bp, seq, d), vb.reshape(bp, seq, d),
                               sb_heads, blk, blk, 0)
            q, k, v, kb, vb = _qkv_proj(xs, g_mix, w_bf, scale, tm_s)
            t_k = -(-(past + dec) // blk) * blk
            k_all, v_all = _sb_keys(cache_sb_k, cache_sb_v, j, kb.reshape(bs, dec, d), vb.reshape(bs, dec, d), t_k)
            a_s = _sb_attention(q.reshape(bs, dec, d), k_all, v_all, sb_heads, dec, blk, past)
            outs["sb_ks"].append(k.reshape(bs, dec, sb_heads, sb_dh))
            outs["sb_vs"].append(v.reshape(bs, dec, sb_heads, sb_dh))
            w_o = sb_w_o[j]
        elif kind == 1:
            w_bf = band_w_qkv[j].astype(BF16)
            scale = band_dh ** -0.5 * LOG2E
            win = BAND_CHUNKS * CHUNK
            q, k, v, kb, vb = _qkv_proj(xp, g_mix, w_bf, scale, tm_p, ones_v=True)
            tile = _band_tile(band_rel_bias[j], win, blk, 0, 3 * blk, 3 * blk) * LOG2E
            ap = _band_attention(q.reshape(bp, seq, d), kb.reshape(bp, seq, d), vb.reshape(bp, seq, 2 * d), tile,
                                 band_heads, blk, blk, True)
            keep = min(win, seq)
            outs["band_kp"].append(k.reshape(bp, seq, d)[:, seq - keep:].reshape(bp, keep, band_heads, band_dh))
            outs["band_vp"].append(v.reshape(bp, seq, d)[:, seq - keep:].reshape(bp, keep, band_heads, band_dh))
            q, k, v, kb, vb = _qkv_proj(xs, g_mix, w_bf, scale, tm_s, ones_v=True)
            n_keys = band_rows + dec
            k_all = _pad_rows(jnp.concatenate([cache_band_k[j].reshape(bs, band_rows, d).astype(BF16),
                                               kb.reshape(bs, dec, d)], axis=1), 3 * blk)
            cache_v = cache_band_v[j].reshape(bs, band_rows, d // 128, 128).astype(BF16)
            cache_v = jnp.concatenate([cache_v, jnp.ones_like(cache_v)], axis=-1).reshape(bs, band_rows, 2 * d)
            v_all = _pad_rows(jnp.concatenate([cache_v, vb.reshape(bs, dec, 2 * d)], axis=1), 3 * blk)
            tile = _band_tile(band_rel_bias[j], past, dec, past - band_rows, 3 * blk, n_keys) * LOG2E
            a_s = _band_attention(q.reshape(bs, dec, d), k_all, v_all, tile, band_heads, dec, blk, False)
            outs["band_ks"].append(k.reshape(bs, dec, band_heads, band_dh))
            outs["band_vs"].append(v.reshape(bs, dec, band_heads, band_dh))
            w_o = band_w_o[j]
        else:
            wts = _mla_weights(mla_w_dq[j], mla_q_norm[j], mla_w_uq[j], mla_w_dkv[j], mla_kv_norm[j],
                               mla_w_uk[j], mla_w_uv[j], mla_heads)
            cos_p, sin_p = _rope_tables(jnp.arange(seq))
            qc, kc, vv, ckv, kr = _mla_proj(xp, g_mix, wts, cos_p, sin_p, mla_heads, mla_scale, tm_p)
            ap = _mla_attention(qc.reshape(bp, seq, -1), kc.reshape(bp, seq, -1), vv.reshape(bp, seq, -1),
                                mla_heads, mla_blk, mla_blk, 0, seq)
            outs["mla_cp"].append(ckv.reshape(bp, seq, lora))
            outs["mla_rp"].append(kr.reshape(bp, seq, MLA_ROPE))
            cos_s, sin_s = _rope_tables(jnp.tile(past + jnp.arange(dec), bs))
            qc, kc, vv, ckv, kr = _mla_proj(xs, g_mix, wts, cos_s, sin_s, mla_heads, mla_scale, tm_s)
            a_s = _mla_sample_attention(qc.reshape(bs, dec, -1), cache_mla_ckv, cache_mla_krope, j,
                                        ckv.reshape(bs, dec, lora), kr.reshape(bs, dec, MLA_ROPE),
                                        wts[6], wts[7], mla_heads)
            outs["mla_cs"].append(ckv.reshape(bs, dec, lora))
            outs["mla_rs"].append(kr.reshape(bs, dec, MLA_ROPE))
            w_o = mla_w_o[j]

        wgu, wd = w_gate_up[layer].astype(BF16), w_down[layer].astype(BF16)
        final = layer == depth - 1
        wo_bf = w_o.astype(BF16)
        xp = _oproj_ffn(xp, ap.reshape(bp * seq, d), wo_bf, norm_ffn[layer], wgu, wd, norm_final, final, tm_ffn, fc)
        xs = _oproj_ffn(xs, a_s.reshape(bs * dec, d), wo_bf, norm_ffn[layer], wgu, wd, norm_final, final, tm_s, fc)

    st = lambda name: jnp.stack(outs[name])
    return (xp.reshape(bp, seq, d), xs.reshape(bs, dec, d),
            sb_kp.reshape(n_sb, bp, seq, sb_heads, sb_dh), sb_vp.reshape(n_sb, bp, seq, sb_heads, sb_dh),
            st("band_kp"), st("band_vp"), st("mla_cp"), st("mla_rp"),
            st("sb_ks"), st("sb_vs"), st("band_ks"), st("band_vs"), st("mla_cs"), st("mla_rs"))
```
